```python
import jax, jax.numpy as jnp
from jax import lax
import numpy as np

D_MODEL = 1024
BATCH = 16
SEQ = 256
DEPTH = 4
DEC_BATCH = 8
DEC_SEQ = 1024
PAST_LEN = 256

GRID_W = 64
A_W = D_MODEL // 2
NA_HEADS = 8
NA_DH = (D_MODEL // 2) // NA_HEADS
NA_W = NA_HEADS * NA_DH
KH_MAX = 8
KW = 16
QB = 16
KB = QB + KW
POOL_W = D_MODEL // 2
POOL_WINDOWS = (2, 4, 8, 16)
N_POOL_GROUPS = 4
POOL_G = POOL_W // N_POOL_GROUPS
MLA_HEADS = 8
NOPE = 64
ROPE = 32
QK_DIM = NOPE + ROPE
V_DIM = 64
Q_LORA = 384
KV_LORA = 256
ROPE_BASE = 10000.0
MLP_HIDDEN = 4 * D_MODEL
N_EVEN = (DEPTH + 1) // 2
N_ODD = DEPTH // 2
Q_BLOCK = 128
DENSE_KEY_LIMIT = 1024
NEG_INF = -1e30
EPS = 1e-6

kernel_name = "hybrid_dit_prefix_ctx_step"


def rms_norm(x, g):
    xf = x.astype(jnp.float32)
    y = xf * lax.rsqrt(jnp.mean(xf * xf, axis=-1, keepdims=True) + EPS)
    return (y * g.astype(jnp.float32)).astype(x.dtype)


def ada_mod(s, w, b):
    m = (s @ w + b)[:, None, :]
    return jnp.split(m, 6, axis=-1)


def modulate(x, g, shift, scale):
    return rms_norm(x, g) * (1 + scale) + shift


def sq_relu_mlp(h, w1, w2):
    a = jax.nn.relu(h @ w1)
    return (a * a) @ w2


def short_conv(u, w):
    up = jnp.pad(u, ((0, 0), (1, 1), (0, 0)))
    return up[:, :-2] * w[0] + up[:, 1:-1] * w[1] + up[:, 2:] * w[2]


def pool_mixer(u, w_groups, scale):
    B, L, _ = u.shape
    ug = u.reshape(B, L, N_POOL_GROUPS, POOL_G)
    cs = jnp.pad(jnp.cumsum(ug.astype(jnp.float32), axis=1), ((0, 0), (1, 0), (0, 0), (0, 0)))
    t = jnp.arange(L)
    means = []
    for g, w in enumerate(POOL_WINDOWS):
        lo = jnp.clip(t - w // 2, 0, L)
        hi = jnp.clip(t - w // 2 + w, 0, L)
        s = cs[:, hi, g] - cs[:, lo, g]
        means.append(s / (hi - lo).astype(jnp.float32)[None, :, None])
    pooled = jnp.stack(means, axis=2).astype(u.dtype)
    y = jnp.einsum('blgc,gcd->blgd', pooled - ug, w_groups).reshape(B, L, POOL_W)
    return y * scale


def axial_rope(n):
    t = jnp.arange(n)
    row = (t // GRID_W).astype(jnp.float32)
    col = (t % GRID_W).astype(jnp.float32)
    axis_dim = ROPE // 2
    inv = 1.0 / (ROPE_BASE ** (jnp.arange(0, axis_dim, 2, dtype=jnp.float32) / axis_dim))
    ang = jnp.concatenate([row[:, None] * inv, col[:, None] * inv], axis=-1)
    return jnp.cos(ang), jnp.sin(ang)


def apply_rope(x, cos, sin):
    x1, x2 = x[..., :ROPE // 2], x[..., ROPE // 2:]
    c = cos[None, :, None, :].astype(x.dtype)
    s = sin[None, :, None, :].astype(x.dtype)
    return jnp.concatenate([x1 * c - x2 * s, x2 * c + x1 * s], axis=-1)


def rope_heads(x, cos, sin):
    return jnp.concatenate([x[..., :NOPE], apply_rope(x[..., NOPE:], cos, sin)], axis=-1)


def dense_attn(q, k, v):
    B, Lq, H, d = q.shape
    Lk = k.shape[1]
    scale = d ** -0.5

    def attend(qb):
        s = jnp.einsum('bqhd,bkhd->bhqk', qb, k).astype(jnp.float32) * scale
        p = jax.nn.softmax(s, axis=-1).astype(v.dtype)
        return jnp.einsum('bhqk,bkhd->bqhd', p, v)

    if Lk < DENSE_KEY_LIMIT:
        return attend(q)
    nb = Lq // Q_BLOCK
    qb = jnp.moveaxis(q.reshape(B, nb, Q_BLOCK, H, d), 1, 0)
    out = lax.map(attend, qb)
    return jnp.moveaxis(out, 0, 1).reshape(B, Lq, H, v.shape[-1])


def na_latent(q, k, v, k_ctx, v_ctx, rpb):
    B, N, H, dh = q.shape
    R = N // GRID_W
    KH = min(KH_MAX, R)
    NCB = GRID_W // QB
    scale = dh ** -0.5
    qg = q.reshape(B, R, NCB, QB, H, dh)
    kg = k.reshape(B, R, GRID_W, H, dh)
    vg = v.reshape(B, R, GRID_W, H, dh)
    r = jnp.arange(R)
    rows_idx = jnp.clip(r - KH // 2, 0, R - KH)[:, None] + jnp.arange(KH)
    col_start = jnp.clip(jnp.arange(GRID_W) - KW // 2, 0, GRID_W - KW)
    j = jnp.arange(NCB)
    cols_idx = jnp.clip(j * QB - KW // 2, 0, GRID_W - KB)[:, None] + jnp.arange(KB)
    kb = kg[:, rows_idx][:, :, :, cols_idx]
    vb = vg[:, rows_idx][:, :, :, cols_idx]
    s_loc = jnp.einsum('brcqhd,brkcjhd->bhrcqkj', qg, kb).astype(jnp.float32) * scale
    qcol = (j * QB)[:, None] + jnp.arange(QB)
    qcs = col_start[qcol]
    keycol = cols_idx[:, None, :]
    valid = (keycol >= qcs[:, :, None]) & (keycol < qcs[:, :, None] + KW)
    dr = rows_idx - r[:, None] + KH_MAX - 1
    dc = jnp.clip(keycol - qcol[:, :, None] + KW - 1, 0, 2 * KW - 2)
    bias = rpb[:, dr[:, None, None, :, None], dc[None, :, :, None, :]]
    s_loc = s_loc + bias[None].astype(jnp.float32)
    s_loc = jnp.where(valid[:, :, None, :], s_loc, NEG_INF).reshape(B, H, R, NCB, QB, KH * KB)
    s_ctx = jnp.einsum('brcqhd,bmhd->bhrcqm', qg, k_ctx).astype(jnp.float32) * scale
    p = jax.nn.softmax(jnp.concatenate([s_loc, s_ctx], axis=-1), axis=-1)
    p_loc = p[..., :KH * KB].reshape(B, H, R, NCB, QB, KH, KB).astype(v.dtype)
    p_ctx = p[..., KH * KB:].astype(v.dtype)
    out = (jnp.einsum('bhrcqkj,brkcjhd->brcqhd', p_loc, vb)
           + jnp.einsum('bhrcqm,bmhd->brcqhd', p_ctx, v_ctx))
    return out.reshape(B, N, H * dh)


def even_proj(h, w_in, conv_w, gq, gk):
    B, L, _ = h.shape
    bg, cg, xa, q, k, v = jnp.split(h @ w_in, 6, axis=-1)
    a = bg * short_conv(cg * xa, conv_w)
    q = rms_norm(q.reshape(B, L, NA_HEADS, NA_DH), gq)
    k = rms_norm(k.reshape(B, L, NA_HEADS, NA_DH), gk)
    v = v.reshape(B, L, NA_HEADS, NA_DH)
    return a, q, k, v


def odd_proj(h, w_in, pool_w, pool_scale, q_a_norm, w_q_b, kv_a_norm, gq):
    B, L, _ = h.shape
    u, q_lat, kv_lat, kpe = jnp.split(
        h @ w_in, [POOL_W, POOL_W + Q_LORA, POOL_W + Q_LORA + KV_LORA], axis=-1)
    p = pool_mixer(u, pool_w, pool_scale)
    q = (rms_norm(q_lat, q_a_norm) @ w_q_b).reshape(B, L, MLA_HEADS, QK_DIM)
    q = rms_norm(q, gq)
    ckv = rms_norm(kv_lat, kv_a_norm)
    return p, q, ckv, kpe


def mla_kv(ckv, kpe, w_kv_b, gk):
    B, L, _ = ckv.shape
    kv = (ckv @ w_kv_b).reshape(B, L, MLA_HEADS, NOPE + V_DIM)
    k_nope, v = kv[..., :NOPE], kv[..., NOPE:]
    k = jnp.concatenate(
        [k_nope, jnp.broadcast_to(kpe[:, :, None, :], (B, L, MLA_HEADS, ROPE))], axis=-1)
    return rms_norm(k, gk), v


def setup_inputs(seed: int = 0) -> dict:
    key = jax.random.key(seed)
    ks = jax.random.split(key, 32)
    f32 = jnp.float32

    def nrm(k, shape, scale=1.0):
        return jax.random.normal(k, shape, f32) * scale

    def gain(k, shape):
        return 1.0 + 0.1 * jax.random.normal(k, shape, f32)

    D = D_MODEL
    even_in = 3 * A_W + 3 * NA_W
    odd_in = POOL_W + Q_LORA + KV_LORA + ROPE
    return {
        "x_prompt": nrm(ks[0], (BATCH, SEQ, D)),
        "x_sample": nrm(ks[1], (DEC_BATCH, DEC_SEQ, D)),
        "cache_na_k": nrm(ks[2], (DEC_BATCH, N_EVEN, PAST_LEN, NA_HEADS, NA_DH)),
        "cache_na_v": nrm(ks[3], (DEC_BATCH, N_EVEN, PAST_LEN, NA_HEADS, NA_DH)),
        "cache_mla_ckv": nrm(ks[4], (DEC_BATCH, N_ODD, PAST_LEN, KV_LORA)),
        "cache_mla_kpe": nrm(ks[5], (DEC_BATCH, N_ODD, PAST_LEN, ROPE)),
        "c": nrm(ks[6], (DEC_BATCH, D)),
        "c_ctx": nrm(ks[7], (D,)),
        "ada_w": nrm(ks[8], (DEPTH, D, 6 * D), 0.5 * D ** -0.5),
        "ada_b": nrm(ks[9], (DEPTH, 6 * D), 0.02),
        "norm1_g": gain(ks[10], (DEPTH, D)),
        "norm2_g": gain(ks[11], (DEPTH, D)),
        "mlp_w1": nrm(ks[12], (DEPTH, D, MLP_HIDDEN), D ** -0.5),
        "mlp_w2": nrm(ks[13], (DEPTH, MLP_HIDDEN, D), MLP_HIDDEN ** -0.5),
        "even_w_in": nrm(ks[14], (N_EVEN, D, even_in), D ** -0.5),
        "even_conv_w": nrm(ks[15], (N_EVEN, 3, A_W), 3 ** -0.5),
        "na_q_norm": gain(ks[16], (N_EVEN, NA_DH)),
        "na_k_norm": gain(ks[17], (N_EVEN, NA_DH)),
        "na_rpb": nrm(ks[18], (N_EVEN, NA_HEADS, 2 * KH_MAX - 1, 2 * KW - 1), 0.1),
        "even_w_out": nrm(ks[19], (N_EVEN, A_W + NA_W, D), (A_W + NA_W) ** -0.5),
        "odd_w_in": nrm(ks[20], (N_ODD, D, odd_in), D ** -0.5),
        "pool_w": nrm(ks[21], (N_ODD, N_POOL_GROUPS, POOL_G, POOL_G), POOL_G ** -0.5),
        "pool_scale": gain(ks[22], (N_ODD, POOL_W)),
        "q_a_norm": gain(ks[23], (N_ODD, Q_LORA)),
        "w_q_b": nrm(ks[24], (N_ODD, Q_LORA, MLA_HEADS * QK_DIM), Q_LORA ** -0.5),
        "kv_a_norm": gain(ks[25], (N_ODD, KV_LORA)),
        "w_kv_b": nrm(ks[26], (N_ODD, KV_LORA, MLA_HEADS * (NOPE + V_DIM)), KV_LORA ** -0.5),
        "mla_q_norm": gain(ks[27], (N_ODD, QK_DIM)),
        "mla_k_norm": gain(ks[28], (N_ODD, QK_DIM)),
        "odd_w_out": nrm(ks[29], (N_ODD, POOL_W + MLA_HEADS * V_DIM, D), (POOL_W + MLA_HEADS * V_DIM) ** -0.5),
    }


def reference(x_prompt, x_sample, cache_na_k, cache_na_v, cache_mla_ckv, cache_mla_kpe, c, c_ctx,
              ada_w, ada_b, norm1_g, norm2_g, mlp_w1, mlp_w2,
              even_w_in, even_conv_w, na_q_norm, na_k_norm, na_rpb, even_w_out,
              odd_w_in, pool_w, pool_scale, q_a_norm, w_q_b, kv_a_norm, w_kv_b,
              mla_q_norm, mla_k_norm, odd_w_out):
    Bp, Lp, _ = x_prompt.shape
    Bs, Ls, _ = x_sample.shape
    cos, sin = axial_rope(Ls)
    s_ctx = jax.nn.silu(c_ctx)[None, :]
    s_lat = jax.nn.silu(c)
    yp, ys = x_prompt, x_sample
    new_na_k, new_na_v, new_ckv, new_kpe = [], [], [], []
    for l in range(DEPTH):
        sh1p, sc1p, g1p, sh2p, sc2p, g2p = ada_mod(s_ctx, ada_w[l], ada_b[l])
        sh1s, sc1s, g1s, sh2s, sc2s, g2s = ada_mod(s_lat, ada_w[l], ada_b[l])
        hp = modulate(yp, norm1_g[l], sh1p, sc1p)
        hs = modulate(ys, norm1_g[l], sh1s, sc1s)
        i = l // 2
        if l % 2 == 0:
            ap, qp, kp, vp = even_proj(hp, even_w_in[i], even_conv_w[i], na_q_norm[i], na_k_norm[i])
            att_p = dense_attn(qp, kp, vp).reshape(Bp, Lp, NA_W)
            mix_p = jnp.concatenate([ap, att_p], axis=-1) @ even_w_out[i]
            new_na_k.append(kp)
            new_na_v.append(vp)
            a_s, qs, k_s, vs = even_proj(hs, even_w_in[i], even_conv_w[i], na_q_norm[i], na_k_norm[i])
            att_s = na_latent(qs, k_s, vs, cache_na_k[:, i], cache_na_v[:, i], na_rpb[i])
            mix_s = jnp.concatenate([a_s, att_s], axis=-1) @ even_w_out[i]
        else:
            pp, qp, ckvp, kpep = odd_proj(hp, odd_w_in[i], pool_w[i], pool_scale[i], q_a_norm[i],
                                          w_q_b[i], kv_a_norm[i], mla_q_norm[i])
            kp, vp = mla_kv(ckvp, kpep, w_kv_b[i], mla_k_norm[i])
            att_p = dense_attn(qp, kp, vp).reshape(Bp, Lp, MLA_HEADS * V_DIM)
            mix_p = jnp.concatenate([pp, att_p], axis=-1) @ odd_w_out[i]
            new_ckv.append(ckvp)
            new_kpe.append(kpep)
            ps, qs, ckvs, kpes = odd_proj(hs, odd_w_in[i], pool_w[i], pool_scale[i], q_a_norm[i],
                                          w_q_b[i], kv_a_norm[i], mla_q_norm[i])
            k_s, vs = mla_kv(ckvs, kpes, w_kv_b[i], mla_k_norm[i])
            qs = rope_heads(qs, cos, sin)
            k_s = rope_heads(k_s, cos, sin)
            kc, vc = mla_kv(cache_mla_ckv[:, i], cache_mla_kpe[:, i], w_kv_b[i], mla_k_norm[i])
            att_s = dense_attn(qs, jnp.concatenate([kc, k_s], axis=1),
                               jnp.concatenate([vc, vs], axis=1)).reshape(Bs, Ls, MLA_HEADS * V_DIM)
            mix_s = jnp.concatenate([ps, att_s], axis=-1) @ odd_w_out[i]
        yp = yp + g1p * mix_p
        ys = ys + g1s * mix_s
        yp = yp + g2p * sq_relu_mlp(modulate(yp, norm2_g[l], sh2p, sc2p), mlp_w1[l], mlp_w2[l])
        ys = ys + g2s * sq_relu_mlp(modulate(ys, norm2_g[l], sh2s, sc2s), mlp_w1[l], mlp_w2[l])
    return (yp, ys, jnp.stack(new_na_k, axis=1), jnp.stack(new_na_v, axis=1),
            jnp.stack(new_ckv, axis=1), jnp.stack(new_kpe, axis=1))
```

```python
import functools

import jax
import jax.numpy as jnp
from jax import lax
from jax.experimental import pallas as pl
from jax.experimental.pallas import tpu as pltpu

F32 = jnp.float32
BF16 = jnp.bfloat16

D = 1024
DEPTH = 4
GRID_W = 64
GRID_R = 16
A_W = 512
NA_HEADS = 8
NA_DH = 64
KH = 8
KW = 16
POOL_WINDOWS = (2, 4, 8, 16)
POOL_G = 128
POOL_PAD = 16
MLA_HEADS = 8
NOPE = 64
ROPE = 32
QK_DIM = NOPE + ROPE
MLA_HP = 128
V_DIM = 64
Q_LORA = 384
KV_LORA = 256
ROPE_BASE = 10000.0
MLP_HIDDEN = 4 * D
NEG_INF = -1e30
EPS = 1e-6
MOD_ROWS = 16
LANES = 128
NA_WIN_ROWS = 12
Q_CHUNK = 256
VMEM_LIMIT = 56 * 1024 * 1024


def _dot(a, b):
    return jnp.dot(a, b, preferred_element_type=F32)


def _dot_nt(a, b):
    return lax.dot_general(a, b, (((1,), (1,)), ((), ())), preferred_element_type=F32)


def _rms(x, g):
    ms = jnp.mean(x * x, axis=-1, keepdims=True)
    return x * lax.rsqrt(ms + EPS) * g


def _modulated(x, g, shift, scale):
    return _rms(x, g) * (1.0 + scale) + shift


def _head_rms_64(x, g):
    t_rows = x.shape[0]
    low = lax.broadcasted_iota(jnp.int32, (t_rows, LANES), 1) < NA_DH
    cols = []
    for j in range(x.shape[1] // LANES):
        t = x[:, j * LANES:(j + 1) * LANES]
        t2 = t * t
        s_lo = jnp.sum(jnp.where(low, t2, 0.0), axis=-1, keepdims=True)
        s_hi = jnp.sum(jnp.where(low, 0.0, t2), axis=-1, keepdims=True)
        r_lo = lax.rsqrt(s_lo * (1.0 / NA_DH) + EPS)
        r_hi = lax.rsqrt(s_hi * (1.0 / NA_DH) + EPS)
        cols.append(t * jnp.where(low, r_lo, r_hi))
    return jnp.concatenate(cols, axis=-1) * g


def _rope_col(t, rc, rsa, rsb):
    return t * rc + pltpu.roll(t, LANES - ROPE // 2, 1) * rsa + pltpu.roll(t, ROPE // 2, 1) * rsb


def _head_rms_mla(x, g, rope_tabs):
    cols = []
    for j in range(MLA_HEADS):
        t = x[:, j * MLA_HP:(j + 1) * MLA_HP]
        ms = jnp.sum(t * t, axis=-1, keepdims=True) * (1.0 / QK_DIM)
        t = t * lax.rsqrt(ms + EPS) * g[:, j * MLA_HP:(j + 1) * MLA_HP]
        if rope_tabs is not None:
            t = _rope_col(t, *rope_tabs)
        cols.append(t)
    return jnp.concatenate(cols, axis=-1)


def _kv_up(ckv, kpe_slot, wkvb_ref, gk, rope_tabs):
    kvb = _dot(ckv.astype(BF16), wkvb_ref[...])
    k_pre = jnp.concatenate(
        [kvb[:, j * MLA_HP:(j + 1) * MLA_HP] + kpe_slot for j in range(MLA_HEADS)], axis=-1)
    k = _head_rms_mla(k_pre, gk, rope_tabs)
    v = kvb[:, MLA_HEADS * MLA_HP:]
    return k, v


def _ada_kernel(c_ref, w_ref, b_ref, o_ref):
    c = c_ref[...]
    s = c * (1.0 / (1.0 + jnp.exp(-c)))
    o_ref[...] = _dot(s.astype(BF16), w_ref[...].astype(BF16)) + b_ref[...]


def _even_in_kernel(y_ref, mod_ref, g1_ref, w_ref, cw_ref, gq_ref, gk_ref,
                    a_ref, q_ref, k_ref, v_ref, *, seq):
    mod = mod_ref[...]
    h = _modulated(y_ref[...], g1_ref[...], mod[:, 0:D], mod[:, D:2 * D])
    z = _dot(h.astype(BF16), w_ref[...])
    u = z[:, A_W:2 * A_W] * z[:, 2 * A_W:3 * A_W]
    t = lax.broadcasted_iota(jnp.int32, (seq, A_W), 0)
    prev = jnp.where(t > 0, pltpu.roll(u, 1, 0), 0.0)
    nxt = jnp.where(t < seq - 1, pltpu.roll(u, seq - 1, 0), 0.0)
    cw = cw_ref[...]
    a = z[:, 0:A_W] * (prev * cw[0:1] + u * cw[1:2] + nxt * cw[2:3])
    a_ref[...] = a.astype(BF16)
    q = _head_rms_64(z[:, 3 * A_W:4 * A_W], gq_ref[...]) * (NA_DH ** -0.5)
    q_ref[...] = q.astype(BF16)
    k_ref[...] = _head_rms_64(z[:, 4 * A_W:5 * A_W], gk_ref[...]).astype(k_ref.dtype)
    v_ref[...] = z[:, 5 * A_W:6 * A_W].astype(v_ref.dtype)


def _pool_branch(u, pw_ref, ps, seq):
    zpad = jnp.zeros((POOL_PAD, POOL_G), F32)
    padded = seq + 2 * POOL_PAD
    t = lax.broadcasted_iota(jnp.int32, (seq, POOL_G), 0)
    cols = []
    for g, w in enumerate(POOL_WINDOWS):
        ug = u[:, g * POOL_G:(g + 1) * POOL_G]
        xp = jnp.concatenate([zpad, ug, zpad], axis=0)
        acc = xp + pltpu.roll(xp, 1, 0)
        s = 1
        while 4 * s <= w:
            acc = pltpu.roll(acc, s, 0) + pltpu.roll(acc, padded - s, 0)
            s *= 2
        win = acc[POOL_PAD:POOL_PAD + seq]
        lo = jnp.maximum(t - w // 2, 0)
        hi = jnp.minimum(t - w // 2 + w, seq)
        pooled = win / (hi - lo).astype(F32)
        cols.append(_dot((pooled - ug).astype(BF16), pw_ref[g]))
    return jnp.concatenate(cols, axis=-1) * ps


def _odd_in_kernel(*refs, seq, rope, states):
    (y_ref, mod_ref, g1_ref, w_ref, pw_ref, ps_ref, qan_ref, wqb_ref, kvan_ref, wkvb_ref,
     gq_ref, gk_ref) = refs[:12]
    pos = 12
    rope_tabs = None
    if rope:
        rope_tabs = tuple(r[...] for r in refs[pos:pos + 3])
        pos += 3
    p_ref, q_ref, k_ref, v_ref = refs[pos:pos + 4]
    pos += 4
    mod = mod_ref[...]
    h = _modulated(y_ref[...], g1_ref[...], mod[:, 0:D], mod[:, D:2 * D])
    z = _dot(h.astype(BF16), w_ref[...])
    p_ref[...] = _pool_branch(z[:, 0:A_W], pw_ref, ps_ref[...], seq).astype(BF16)
    o_q, o_kv, o_pe = A_W, A_W + Q_LORA, A_W + Q_LORA + KV_LORA
    ql = _rms(z[:, o_q:o_kv], qan_ref[...])
    qf = _dot(ql.astype(BF16), wqb_ref[...])
    q = _head_rms_mla(qf, gq_ref[...], rope_tabs) * (QK_DIM ** -0.5)
    q_ref[...] = q.astype(BF16)
    ckv = _rms(z[:, o_kv:o_pe], kvan_ref[...])
    kpe_slot = z[:, o_pe:o_pe + MLA_HP]
    k, v = _kv_up(ckv, kpe_slot, wkvb_ref, gk_ref[...], rope_tabs)
    k_ref[...] = k.astype(BF16)
    v_ref[...] = v.astype(BF16)
    if states:
        ckv_ref, kpe_ref = refs[pos:pos + 2]
        ckv_ref[...] = ckv
        kpe_ref[...] = kpe_slot


def _cache_kv_kernel(ckv_ref, kpe_ref, wkvb_ref, gk_ref, k_ref, v_ref):
    k, v = _kv_up(ckv_ref[...], kpe_ref[...], wkvb_ref, gk_ref[...], None)
    k_ref[...] = k.astype(BF16)
    v_ref[...] = v.astype(BF16)


def _attend(q, sources):
    scores = []
    for k, _, bias in sources:
        s = _dot_nt(q, k)
        scores.append(s if bias is None else s + bias)
    m = scores[0].max(axis=-1, keepdims=True)
    for s in scores[1:]:
        m = jnp.maximum(m, s.max(axis=-1, keepdims=True))
    den = None
    out = None
    for s, (_, v, _) in zip(scores, sources):
        p = jnp.exp(s - m)
        ps = p.sum(axis=-1, keepdims=True)
        pv = _dot(p.astype(BF16), v)
        den = ps if den is None else den + ps
        out = pv if out is None else out + pv
    return out / den


def _pair_outputs(q_ref, q_rows, pair, dk, srcs, n_rows):
    low = lax.broadcasted_iota(jnp.int32, (n_rows, LANES), 1) < V_DIM
    vcols = slice(pair * LANES, (pair + 1) * LANES)
    outs = []
    for hh in range(2):
        if dk == NA_DH:
            q128 = q_ref[q_rows, pair * LANES:(pair + 1) * LANES]
            keep = low if hh == 0 else jnp.logical_not(low)
            q = jnp.where(keep, q128, jnp.zeros_like(q128))
            kcols = slice(pair * LANES, (pair + 1) * LANES)
        else:
            kcols = slice((2 * pair + hh) * dk, (2 * pair + hh + 1) * dk)
            q = q_ref[q_rows, kcols]
        sources = []
        for k_ref, k_rows, v_ref, v_rows, bias_pair in srcs:
            k = k_ref[k_rows, kcols].astype(BF16)
            v = v_ref[v_rows, vcols].astype(BF16)
            sources.append((k, v, None if bias_pair is None else bias_pair[hh]))
        outs.append(_attend(q, sources))
    return jnp.where(low, outs[0], outs[1])


def _attn_kernel(*refs, lq, dk, n_pairs, n_src):
    q_ref = refs[0]
    kv = refs[1:1 + 2 * n_src]
    o_ref = refs[1 + 2 * n_src]
    full = slice(None)
    for c in range(lq // Q_CHUNK):
        rows = slice(c * Q_CHUNK, (c + 1) * Q_CHUNK)
        for pair in range(n_pairs):
            srcs = [(kv[2 * i], full, kv[2 * i + 1], full, None) for i in range(n_src)]
            o = _pair_outputs(q_ref, rows, pair, dk, srcs, Q_CHUNK)
            o_ref[rows, pair * LANES:(pair + 1) * LANES] = o.astype(o_ref.dtype)


def _na_kernel(q_ref, k_ref, v_ref, ck_ref, cv_ref, pb_ref, o_ref, bias_ref):
    @pl.when(pl.program_id(1) == 0)
    def _():
        low = lax.broadcasted_iota(jnp.int32, (GRID_W, LANES), 1) < GRID_W
        neg = jnp.full((GRID_W, LANES), NEG_INF, F32)
        for hh in range(2):
            for r in range(GRID_R):
                win0 = (r // 8) * 4
                rs = min(max(r - KH // 2, 0), GRID_R - KH)
                delta, dr0 = rs - win0, rs - r + KH - 1
                for p in range(NA_WIN_ROWS // 2):
                    ok0 = delta <= 2 * p < delta + KH
                    ok1 = delta <= 2 * p + 1 < delta + KH
                    if not (ok0 or ok1):
                        blk = neg
                    else:
                        blk = pb_ref[hh, dr0 + 2 * p - delta + 1]
                        if not ok0:
                            blk = jnp.where(low, NEG_INF, blk)
                        if not ok1:
                            blk = jnp.where(low, blk, NEG_INF)
                    bias_ref[hh, r * GRID_W:(r + 1) * GRID_W, p * LANES:(p + 1) * LANES] = blk

    full = slice(None)
    for c in range(GRID_R * GRID_W // Q_CHUNK):
        rows = slice(c * Q_CHUNK, (c + 1) * Q_CHUNK)
        win = slice((c // 2) * 4 * GRID_W, ((c // 2) * 4 + NA_WIN_ROWS) * GRID_W)
        bias_pair = (bias_ref[0, rows, :], bias_ref[1, rows, :])
        srcs = [(k_ref, win, v_ref, win, bias_pair), (ck_ref, full, cv_ref, full, None)]
        o = _pair_outputs(q_ref, rows, 0, NA_DH, srcs, Q_CHUNK)
        o_ref[rows, :] = o.astype(o_ref.dtype)


def _out_mlp_kernel(y_ref, b1_ref, b2_ref, mod_ref, wo_ref, g2_ref, w1_ref, w2_ref, o_ref, *, hc):
    mod = mod_ref[...]
    mix = _dot(b1_ref[...], wo_ref[0:A_W, :]) + _dot(b2_ref[...], wo_ref[A_W:2 * A_W, :])
    y1 = y_ref[...] + mod[:, 2 * D:3 * D] * mix
    h2 = _modulated(y1, g2_ref[...], mod[:, 3 * D:4 * D], mod[:, 4 * D:5 * D]).astype(BF16)
    acc = jnp.zeros(y1.shape, F32)
    for c in range(MLP_HIDDEN // hc):
        a = jnp.maximum(_dot(h2, w1_ref[:, c * hc:(c + 1) * hc]), 0.0)
        acc = acc + _dot((a * a).astype(BF16), w2_ref[c * hc:(c + 1) * hc, :])
    o_ref[...] = y1 + mod[:, 5 * D:6 * D] * acc


def _params(n_grid):
    return pltpu.CompilerParams(dimension_semantics=("arbitrary",) * n_grid,
                                vmem_limit_bytes=VMEM_LIMIT)


def _layer_spec(shape, layer):
    nd = len(shape)
    return pl.BlockSpec((None,) + tuple(shape[1:]), lambda *_: (layer,) + (0,) * (nd - 1))


def _mod_spec(layer, row_of_batch):
    return pl.BlockSpec((None, None, 1, 6 * D), lambda b, *_: (layer, row_of_batch(b), 0, 0))


def _ada_call(c_all, ada_w, ada_b):
    tn = 1536
    return pl.pallas_call(
        _ada_kernel,
        out_shape=jax.ShapeDtypeStruct((DEPTH, MOD_ROWS, 6 * D), F32),
        grid=(DEPTH, 6 * D // tn),
        in_specs=[pl.BlockSpec((MOD_ROWS, D), lambda l, j: (0, 0)),
                  pl.BlockSpec((None, D, tn), lambda l, j: (l, 0, j)),
                  pl.BlockSpec((None, 1, tn), lambda l, j: (l, 0, j))],
        out_specs=pl.BlockSpec((None, MOD_ROWS, tn), lambda l, j: (l, 0, j)),
        compiler_params=_params(2), name="ada_mod",
    )(c_all, ada_w, ada_b)


def _even_in_call(y, mods, layer, i, row_of_batch, pp, kv_dtype):
    nb, seq, _ = y.shape
    tok = lambda w: pl.BlockSpec((None, seq, w), lambda b: (b, 0, 0))
    return pl.pallas_call(
        functools.partial(_even_in_kernel, seq=seq),
        out_shape=(jax.ShapeDtypeStruct((nb, seq, A_W), BF16),
                   jax.ShapeDtypeStruct((nb, seq, A_W), BF16),
                   jax.ShapeDtypeStruct((nb, seq, A_W), kv_dtype),
                   jax.ShapeDtypeStruct((nb, seq, A_W), kv_dtype)),
        grid=(nb,),
        in_specs=[tok(D), _mod_spec(layer, row_of_batch), _layer_spec(pp["norm1_g"].shape, layer),
                  _layer_spec(pp["even_w_in"].shape, i), _layer_spec(pp["even_conv_w"].shape, i),
                  _layer_spec(pp["na_q_norm"].shape, i), _layer_spec(pp["na_k_norm"].shape, i)],
        out_specs=(tok(A_W), tok(A_W), tok(A_W), tok(A_W)),
        compiler_params=_params(1), name="even_in",
    )(y, mods, pp["norm1_g"], pp["even_w_in"], pp["even_conv_w"], pp["na_q_norm"], pp["na_k_norm"])


def _odd_in_call(y, mods, layer, i, row_of_batch, pp, rope_tabs, states):
    nb, seq, _ = y.shape
    tok = lambda w: pl.BlockSpec((None, seq, w), lambda b: (b, 0, 0))
    names = ["norm1_g", "odd_w_in", "pool_w", "pool_scale", "q_a_norm", "w_q_b", "kv_a_norm",
             "w_kv_b", "mla_q_norm", "mla_k_norm"]
    layers = [layer] + [i] * 9
    in_specs = [tok(D), _mod_spec(layer, row_of_batch)]
    in_specs += [_layer_spec(pp[n].shape, li) for n, li in zip(names, layers)]
    args = [y, mods] + [pp[n] for n in names]
    if rope_tabs is not None:
        in_specs += [pl.BlockSpec((seq, MLA_HP), lambda b: (0, 0))] * 3
        args += list(rope_tabs)
    out_shape = [jax.ShapeDtypeStruct((nb, seq, A_W), BF16),
                 jax.ShapeDtypeStruct((nb, seq, MLA_HEADS * MLA_HP), BF16),
                 jax.ShapeDtypeStruct((nb, seq, MLA_HEADS * MLA_HP), BF16),
                 jax.ShapeDtypeStruct((nb, seq, MLA_HEADS * V_DIM), BF16)]
    out_specs = [tok(A_W), tok(MLA_HEADS * MLA_HP), tok(MLA_HEADS * MLA_HP), tok(MLA_HEADS * V_DIM)]
    if states:
        out_shape += [jax.ShapeDtypeStruct((nb, seq, KV_LORA), F32),
                      jax.ShapeDtypeStruct((nb, seq, MLA_HP), F32)]
        out_specs += [tok(KV_LORA), tok(MLA_HP)]
    return pl.pallas_call(
        functools.partial(_odd_in_kernel, seq=seq, rope=rope_tabs is not None, states=states),
        out_shape=tuple(out_shape), grid=(nb,), in_specs=in_specs, out_specs=tuple(out_specs),
        compiler_params=_params(1), name="odd_in",
    )(*args)


def _cache_kv_call(cache_ckv, cache_kpe_slot, i, pp):
    nb, _, past, _ = cache_ckv.shape
    cache = lambda w: pl.BlockSpec((None, None, past, w), lambda b: (b, i, 0, 0))
    tok = lambda w: pl.BlockSpec((None, past, w), lambda b: (b, 0, 0))
    return pl.pallas_call(
        _cache_kv_kernel,
        out_shape=(jax.ShapeDtypeStruct((nb, past, MLA_HEADS * MLA_HP), BF16),
                   jax.ShapeDtypeStruct((nb, past, MLA_HEADS * V_DIM), BF16)),
        grid=(nb,),
        in_specs=[cache(KV_LORA), cache(MLA_HP), _layer_spec(pp["w_kv_b"].shape, i),
                  _layer_spec(pp["mla_k_norm"].shape, i)],
        out_specs=(tok(MLA_HEADS * MLA_HP), tok(MLA_HEADS * V_DIM)),
        compiler_params=_params(1), name="cache_kv",
    )(cache_ckv, cache_kpe_slot, pp["w_kv_b"], pp["mla_k_norm"])


def _attn_call(q, kvs, dk, pairs_per_step):
    nb, lq, _ = q.shape
    n_groups = NA_HEADS // (2 * pairs_per_step)
    qk_w, v_w = 2 * dk * pairs_per_step, LANES * pairs_per_step
    in_specs = [pl.BlockSpec((None, lq, qk_w), lambda g, b: (b, 0, g))]
    args = [q]
    for k, v in kvs:
        in_specs += [pl.BlockSpec((None, k.shape[1], qk_w), lambda g, b: (b, 0, g)),
                     pl.BlockSpec((None, v.shape[1], v_w), lambda g, b: (b, 0, g))]
        args += [k, v]
    return pl.pallas_call(
        functools.partial(_attn_kernel, lq=lq, dk=dk, n_pairs=pairs_per_step, n_src=len(kvs)),
        out_shape=jax.ShapeDtypeStruct((nb, lq, NA_HEADS * V_DIM), BF16),
        grid=(n_groups, nb), in_specs=in_specs,
        out_specs=pl.BlockSpec((None, lq, v_w), lambda g, b: (b, 0, g)),
        compiler_params=_params(2), name="attn",
    )(*args)


def _na_call(q, k, v, cache_k, cache_v, i, pair_bias):
    nb, lq, _ = q.shape
    past = cache_k.shape[2]
    tok = pl.BlockSpec((None, lq, LANES), lambda g, b: (b, 0, g))
    cache = pl.BlockSpec((None, None, past, LANES), lambda g, b: (b, i, 0, g))
    return pl.pallas_call(
        _na_kernel,
        out_shape=jax.ShapeDtypeStruct((nb, lq, NA_HEADS * NA_DH), BF16),
        grid=(NA_HEADS // 2, nb),
        in_specs=[tok, tok, tok, cache, cache,
                  pl.BlockSpec((None, 2, 2 * KH, GRID_W, LANES), lambda g, b: (i, g, 0, 0, 0))],
        out_specs=tok,
        scratch_shapes=[pltpu.VMEM((2, lq, NA_WIN_ROWS * GRID_W), F32)],
        compiler_params=_params(2), name="na_attn",
    )(q, k, v, cache_k, cache_v, pair_bias)


def _out_mlp_call(y, b1, b2, mods, layer, row_of_batch, w_out, w_out_layer, pp):
    nb, seq, _ = y.shape
    tm = 512
    tok = lambda w: pl.BlockSpec((None, tm, w), lambda b, t: (b, t, 0))
    return pl.pallas_call(
        functools.partial(_out_mlp_kernel, hc=512),
        out_shape=jax.ShapeDtypeStruct(y.shape, F32),
        grid=(nb, seq // tm),
        in_specs=[tok(D), tok(A_W), tok(A_W), _mod_spec(layer, row_of_batch),
                  _layer_spec(w_out.shape, w_out_layer), _layer_spec(pp["norm2_g"].shape, layer),
                  _layer_spec(pp["mlp_w1"].shape, layer), _layer_spec(pp["mlp_w2"].shape, layer)],
        out_specs=tok(D),
        compiler_params=_params(2), name="out_mlp",
    )(y, b1, b2, mods, w_out, pp["norm2_g"], pp["mlp_w1"], pp["mlp_w2"])


def _row(v):
    return v[:, None, :].astype(F32)


def _head_slots(v):
    slot = jnp.pad(v, ((0, 0), (0, MLA_HP - QK_DIM)))
    return _row(jnp.tile(slot, (1, MLA_HEADS)))


def _prepare(p):
    n_odd = p["odd_w_in"].shape[0]
    pp = {}
    pp["norm1_g"], pp["norm2_g"] = _row(p["norm1_g"]), _row(p["norm2_g"])
    pp["mlp_w1"], pp["mlp_w2"] = p["mlp_w1"].astype(BF16), p["mlp_w2"].astype(BF16)
    pp["even_w_in"] = p["even_w_in"].astype(BF16)
    pp["even_conv_w"] = p["even_conv_w"].astype(F32)
    pp["na_q_norm"] = _row(jnp.tile(p["na_q_norm"], (1, NA_HEADS)))
    pp["na_k_norm"] = _row(jnp.tile(p["na_k_norm"], (1, NA_HEADS)))
    pp["even_w_out"] = p["even_w_out"].astype(BF16)
    pp["odd_w_out"] = p["odd_w_out"].astype(BF16)
    w = p["odd_w_in"]
    o_pe = A_W + Q_LORA + KV_LORA
    pe = jnp.pad(w[:, :, o_pe:], ((0, 0), (0, 0), (NOPE, MLA_HP - QK_DIM)))
    pp["odd_w_in"] = jnp.concatenate([w[:, :, :o_pe], pe], axis=-1).astype(BF16)
    pp["pool_w"] = p["pool_w"].astype(BF16)
    pp["pool_scale"] = _row(p["pool_scale"])
    pp["q_a_norm"], pp["kv_a_norm"] = _row(p["q_a_norm"]), _row(p["kv_a_norm"])
    wq = p["w_q_b"].reshape(n_odd, Q_LORA, MLA_HEADS, QK_DIM)
    wq = jnp.pad(wq, ((0, 0), (0, 0), (0, 0), (0, MLA_HP - QK_DIM)))
    pp["w_q_b"] = wq.reshape(n_odd, Q_LORA, MLA_HEADS * MLA_HP).astype(BF16)
    wkv = p["w_kv_b"].reshape(n_odd, KV_LORA, MLA_HEADS, NOPE + V_DIM)
    wk = jnp.pad(wkv[..., :NOPE], ((0, 0), (0, 0), (0, 0), (0, MLA_HP - NOPE)))
    pp["w_kv_b"] = jnp.concatenate(
        [wk.reshape(n_odd, KV_LORA, MLA_HEADS * MLA_HP),
         wkv[..., NOPE:].reshape(n_odd, KV_LORA, MLA_HEADS * V_DIM)], axis=-1).astype(BF16)
    pp["mla_q_norm"], pp["mla_k_norm"] = _head_slots(p["mla_q_norm"]), _head_slots(p["mla_k_norm"])
    return pp


def _rope_tables(n):
    t = jnp.arange(n)
    row = (t // GRID_W).astype(F32)
    col = (t % GRID_W).astype(F32)
    axis_dim = ROPE // 2
    inv = 1.0 / (ROPE_BASE ** (jnp.arange(0, axis_dim, 2, dtype=F32) / axis_dim))
    ang = jnp.concatenate([row[:, None] * inv, col[:, None] * inv], axis=-1)
    cos, sin = jnp.cos(ang), jnp.sin(ang)
    zeros = jnp.zeros((n, axis_dim), F32)
    tail = jnp.zeros((n, MLA_HP - QK_DIM), F32)
    rc = jnp.concatenate([jnp.ones((n, NOPE), F32), cos, cos, tail], axis=-1)
    rsa = jnp.concatenate([jnp.zeros((n, NOPE), F32), -sin, zeros, tail], axis=-1)
    rsb = jnp.concatenate([jnp.zeros((n, NOPE), F32), zeros, sin, tail], axis=-1)
    return rc, rsa, rsb


def _na_pair_bias(rpb):
    c = jnp.arange(GRID_W)
    start = jnp.clip(c - KW // 2, 0, GRID_W - KW)
    kc = jnp.arange(GRID_W)
    valid = (kc[None, :] >= start[:, None]) & (kc[None, :] < start[:, None] + KW)
    dc = jnp.clip(kc[None, :] - c[:, None] + KW - 1, 0, 2 * KW - 2)
    blocks = jnp.where(valid, rpb[:, :, :, dc], NEG_INF)
    blocks = jnp.pad(blocks, ((0, 0), (0, 0), (1, 1), (0, 0), (0, 0)), constant_values=NEG_INF)
    return jnp.concatenate([blocks[:, :, :-1], blocks[:, :, 1:]], axis=-1).astype(F32)


def kernel(x_prompt, x_sample, cache_na_k, cache_na_v, cache_mla_ckv, cache_mla_kpe, c, c_ctx,
           ada_w, ada_b, norm1_g, norm2_g, mlp_w1, mlp_w2,
           even_w_in, even_conv_w, na_q_norm, na_k_norm, na_rpb, even_w_out,
           odd_w_in, pool_w, pool_scale, q_a_norm, w_q_b, kv_a_norm, w_kv_b,
           mla_q_norm, mla_k_norm, odd_w_out):
    bp, lp, _ = x_prompt.shape
    bs, ls, _ = x_sample.shape
    pp = _prepare(dict(norm1_g=norm1_g, norm2_g=norm2_g, mlp_w1=mlp_w1, mlp_w2=mlp_w2,
                       even_w_in=even_w_in, even_conv_w=even_conv_w, na_q_norm=na_q_norm,
                       na_k_norm=na_k_norm, even_w_out=even_w_out, odd_w_in=odd_w_in, pool_w=pool_w,
                       pool_scale=pool_scale, q_a_norm=q_a_norm, w_q_b=w_q_b, kv_a_norm=kv_a_norm,
                       w_kv_b=w_kv_b, mla_q_norm=mla_q_norm, mla_k_norm=mla_k_norm,
                       odd_w_out=odd_w_out))
    c_all = jnp.concatenate(
        [c_ctx[None, :], c, jnp.zeros((MOD_ROWS - 1 - bs, D), F32)], axis=0)
    mods = _ada_call(c_all, ada_w, ada_b[:, None, :]).reshape(DEPTH, MOD_ROWS, 1, 6 * D)
    rope_tabs = _rope_tables(ls)
    pair_bias = _na_pair_bias(na_rpb)
    past = cache_na_k.shape[2]
    cache_k = cache_na_k.reshape(bs, -1, past, NA_HEADS * NA_DH)
    cache_v = cache_na_v.reshape(bs, -1, past, NA_HEADS * NA_DH)
    cache_kpe_slot = jnp.pad(cache_mla_kpe, ((0, 0), (0, 0), (0, 0), (NOPE, MLA_HP - QK_DIM)))
    ctx_row = lambda b: 0
    lat_row = lambda b: b + 1

    yp, ys = x_prompt, x_sample
    na_k, na_v, mla_ckv, mla_kpe = [], [], [], []
    for l in range(DEPTH):
        i = l // 2
        if l % 2 == 0:
            ap, qp, kp, vp = _even_in_call(yp, mods, l, i, ctx_row, pp, F32)
            att_p = _attn_call(qp, [(kp, vp)], NA_DH, 4)
            na_k.append(kp.reshape(bp, lp, NA_HEADS, NA_DH))
            na_v.append(vp.reshape(bp, lp, NA_HEADS, NA_DH))
            a_s, qs, k_s, vs = _even_in_call(ys, mods, l, i, lat_row, pp, BF16)
            att_s = _na_call(qs, k_s, vs, cache_k, cache_v, i, pair_bias)
            w_out = pp["even_w_out"]
        else:
            ap, qp, kp, vp, ckvp, kpep = _odd_in_call(yp, mods, l, i, ctx_row, pp, None, True)
            att_p = _attn_call(qp, [(kp, vp)], MLA_HP, 4)
            mla_ckv.append(ckvp)
            mla_kpe.append(kpep[:, :, NOPE:QK_DIM])
            a_s, qs, k_s, vs = _odd_in_call(ys, mods, l, i, lat_row, pp, rope_tabs, False)
            kc, vc = _cache_kv_call(cache_mla_ckv, cache_kpe_slot, i, pp)
            att_s = _attn_call(qs, [(kc, vc), (k_s, vs)], MLA_HP, 1)
            w_out = pp["odd_w_out"]
        yp = _out_mlp_call(yp.reshape(1, bp * lp, D), ap.reshape(1, bp * lp, A_W),
                           att_p.reshape(1, bp * lp, A_W), mods, l, ctx_row, w_out, i,
                           pp).reshape(bp, lp, D)
        ys = _out_mlp_call(ys, a_s, att_s, mods, l, lat_row, w_out, i, pp)
    return (yp, ys, jnp.stack(na_k, axis=1), jnp.stack(na_v, axis=1),
            jnp.stack(mla_ckv, axis=1), jnp.stack(mla_kpe, axis=1))
```

```python
import functools

import jax
import jax.numpy as jnp
import numpy as np
from jax import lax
from jax.experimental import pallas as pl
from jax.experimental.pallas import tpu as pltpu

F32 = jnp.float32
BF16 = jnp.bfloat16

D = 1024
DEPTH = 4
GRID_W = 64
GRID_R = 16
A_W = 512
NA_HEADS = 8
NA_DH = 64
KH = 8
KW = 16
POOL_WINDOWS = (2, 4, 8, 16)
POOL_G = 128
POOL_PAD = 16
MLA_HEADS = 8
NOPE = 64
ROPE = 32
QK_DIM = NOPE + ROPE
MLA_HP = 128
V_DIM = 64
Q_LORA = 384
KV_LORA = 256
ROPE_BASE = 10000.0
MLP_HIDDEN = 4 * D
NEG_INF = -1e30
EPS = 1e-6
MOD_ROWS = 16
LANES = 128
NA_WIN_ROWS = 12
Q_CHUNK = 256
VMEM_LIMIT = 56 * 1024 * 1024


def _dot(a, b):
    return jnp.dot(a, b, preferred_element_type=F32)


def _dot_nt(a, b):
    return lax.dot_general(a, b, (((1,), (1,)), ((), ())), preferred_element_type=F32)


def _rms(x, g):
    ms = jnp.mean(x * x, axis=-1, keepdims=True)
    return x * lax.rsqrt(ms + EPS) * g


def _modulated(x, g, shift, scale):
    return _rms(x, g) * (1.0 + scale) + shift


def _head_rms_64(x, g):
    t_rows = x.shape[0]
    low = lax.broadcasted_iota(jnp.int32, (t_rows, LANES), 1) < NA_DH
    cols = []
    for j in range(x.shape[1] // LANES):
        t = x[:, j * LANES:(j + 1) * LANES]
        t2 = t * t
        s_lo = jnp.sum(jnp.where(low, t2, 0.0), axis=-1, keepdims=True)
        s_hi = jnp.sum(jnp.where(low, 0.0, t2), axis=-1, keepdims=True)
        r_lo = lax.rsqrt(s_lo * (1.0 / NA_DH) + EPS)
        r_hi = lax.rsqrt(s_hi * (1.0 / NA_DH) + EPS)
        cols.append(t * jnp.where(low, r_lo, r_hi))
    return jnp.concatenate(cols, axis=-1) * g


def _slot_inv_rms(t):
    return lax.rsqrt(jnp.sum(t * t, axis=-1, keepdims=True) * (1.0 / QK_DIM) + EPS)


def _rotary_partner(t):
    x1_side = lax.broadcasted_iota(jnp.int32, t.shape, 1) < NOPE + ROPE // 2
    return jnp.where(x1_side, pltpu.roll(t, LANES - ROPE // 2, 1), pltpu.roll(t, ROPE // 2, 1))


def _mla_q(qf, q_partner, g_slot, g_partner, rope_tabs):
    if rope_tabs is not None:
        rc, rs = rope_tabs
        gc, gs = g_slot * rc, g_partner * rs
    cols = []
    for j in range(MLA_HEADS):
        t = qf[:, j * MLA_HP:(j + 1) * MLA_HP]
        if rope_tabs is None:
            u = t * g_slot
        else:
            u = t * gc + q_partner[:, j * MLA_HP:(j + 1) * MLA_HP] * gs
        cols.append(u * (_slot_inv_rms(t) * (QK_DIM ** -0.5)))
    return jnp.concatenate(cols, axis=-1)


def _kv_up(ckv, kpe_slot, wkvb_ref, g_slot, rope_tabs):
    kvb = _dot(ckv.astype(BF16), wkvb_ref[...])
    kr = kpe_slot * g_slot
    if rope_tabs is not None:
        rc, rs = rope_tabs
        kr = kr * rc + _rotary_partner(kr) * rs
    cols = []
    for j in range(MLA_HEADS):
        kn = kvb[:, j * MLA_HP:(j + 1) * MLA_HP]
        cols.append((kn * g_slot + kr) * _slot_inv_rms(kn + kpe_slot))
    return jnp.concatenate(cols, axis=-1), kvb[:, MLA_HEADS * MLA_HP:]


def _ada_kernel(c_ref, w_ref, b_ref, o_ref):
    c = c_ref[...]
    s = c * (1.0 / (1.0 + jnp.exp(-c)))
    o_ref[...] = _dot(s.astype(BF16), w_ref[...].astype(BF16)) + b_ref[...]


def _even_in_kernel(y_ref, mod_ref, g1_ref, w_ref, cw_ref, gq_ref, gk_ref,
                    a_ref, q_ref, k_ref, v_ref, *, seq):
    mod = mod_ref[...]
    h = _modulated(y_ref[...], g1_ref[...], mod[:, 0:D], mod[:, D:2 * D])
    z = _dot(h.astype(BF16), w_ref[...])
    u = z[:, A_W:2 * A_W] * z[:, 2 * A_W:3 * A_W]
    t = lax.broadcasted_iota(jnp.int32, (seq, A_W), 0)
    prev = jnp.where(t > 0, pltpu.roll(u, 1, 0), 0.0)
    nxt = jnp.where(t < seq - 1, pltpu.roll(u, seq - 1, 0), 0.0)
    cw = cw_ref[...]
    a = z[:, 0:A_W] * (prev * cw[0:1] + u * cw[1:2] + nxt * cw[2:3])
    a_ref[...] = a.astype(BF16)
    q = _head_rms_64(z[:, 3 * A_W:4 * A_W], gq_ref[...]) * (NA_DH ** -0.5)
    q_ref[...] = q.astype(BF16)
    k_ref[...] = _head_rms_64(z[:, 4 * A_W:5 * A_W], gk_ref[...]).astype(k_ref.dtype)
    v_ref[...] = z[:, 5 * A_W:6 * A_W].astype(v_ref.dtype)


def _pool_branch(u, pw_ref, ps, seq):
    zpad = jnp.zeros((POOL_PAD, POOL_G), F32)
    padded = seq + 2 * POOL_PAD
    t = lax.broadcasted_iota(jnp.int32, (seq, POOL_G), 0)
    cols = []
    for g, w in enumerate(POOL_WINDOWS):
        ug = u[:, g * POOL_G:(g + 1) * POOL_G]
        xp = jnp.concatenate([zpad, ug, zpad], axis=0)
        acc = xp + pltpu.roll(xp, 1, 0)
        s = 1
        while 4 * s <= w:
            acc = pltpu.roll(acc, s, 0) + pltpu.roll(acc, padded - s, 0)
            s *= 2
        win = acc[POOL_PAD:POOL_PAD + seq]
        lo = jnp.maximum(t - w // 2, 0)
        hi = jnp.minimum(t - w // 2 + w, seq)
        pooled = win / (hi - lo).astype(F32)
        cols.append(_dot((pooled - ug).astype(BF16), pw_ref[g]))
    return jnp.concatenate(cols, axis=-1) * ps


def _odd_in_kernel(*refs, seq, rope, states):
    (y_ref, mod_ref, g1_ref, w_ref, pw_ref, ps_ref, qan_ref, wqb_ref, kvan_ref, wkvb_ref,
     gq_ref, gk_ref) = refs[:12]
    pos = 12
    rope_tabs = g_partner = q_partner = None
    if rope:
        rope_tabs = (refs[pos][...], refs[pos + 1][...])
        g_partner = refs[pos + 2][...]
        pos += 3
    p_ref, q_ref, k_ref, v_ref = refs[pos:pos + 4]
    pos += 4
    mod = mod_ref[...]
    h = _modulated(y_ref[...], g1_ref[...], mod[:, 0:D], mod[:, D:2 * D])
    z = _dot(h.astype(BF16), w_ref[...])
    p_ref[...] = _pool_branch(z[:, 0:A_W], pw_ref, ps_ref[...], seq).astype(BF16)
    o_q, o_kv, o_pe = A_W, A_W + Q_LORA, A_W + Q_LORA + KV_LORA
    ql = _rms(z[:, o_q:o_kv], qan_ref[...])
    qf = _dot(ql.astype(BF16), wqb_ref[...])
    if rope:
        q_partner = qf[:, MLA_HEADS * MLA_HP:]
    q_ref[...] = _mla_q(qf, q_partner, gq_ref[...], g_partner, rope_tabs).astype(BF16)
    ckv = _rms(z[:, o_kv:o_pe], kvan_ref[...])
    kpe_slot = z[:, o_pe:o_pe + MLA_HP]
    k, v = _kv_up(ckv, kpe_slot, wkvb_ref, gk_ref[...], rope_tabs)
    k_ref[...] = k.astype(BF16)
    v_ref[...] = v.astype(BF16)
    if states:
        ckv_ref, kpe_ref = refs[pos:pos + 2]
        ckv_ref[...] = ckv
        kpe_ref[...] = kpe_slot


def _cache_kv_kernel(ckv_ref, kpe_ref, wkvb_ref, gk_ref, k_ref, v_ref):
    k, v = _kv_up(ckv_ref[...], kpe_ref[...], wkvb_ref, gk_ref[...], None)
    k_ref[...] = k.astype(BF16)
    v_ref[...] = v.astype(BF16)


def _attend(q, sources):
    scores = []
    for k, _, bias in sources:
        s = _dot_nt(q, k)
        scores.append(s if bias is None else s + bias)
    m = scores[0].max(axis=-1, keepdims=True)
    for s in scores[1:]:
        m = jnp.maximum(m, s.max(axis=-1, keepdims=True))
    den = None
    out = None
    for s, (_, v, _) in zip(scores, sources):
        p = jnp.exp(s - m)
        ps = p.sum(axis=-1, keepdims=True)
        pv = _dot(p.astype(BF16), v)
        den = ps if den is None else den + ps
        out = pv if out is None else out + pv
    return out / den


def _pair_outputs(q_ref, q_rows, pair, dk, srcs, n_rows):
    low = lax.broadcasted_iota(jnp.int32, (n_rows, LANES), 1) < V_DIM
    vcols = slice(pair * LANES, (pair + 1) * LANES)
    outs = []
    for hh in range(2):
        if dk == NA_DH:
            q128 = q_ref[q_rows, pair * LANES:(pair + 1) * LANES]
            keep = low if hh == 0 else jnp.logical_not(low)
            q = jnp.where(keep, q128, jnp.zeros_like(q128))
            kcols = slice(pair * LANES, (pair + 1) * LANES)
        else:
            kcols = slice((2 * pair + hh) * dk, (2 * pair + hh + 1) * dk)
            q = q_ref[q_rows, kcols]
        sources = []
        for k_ref, k_rows, v_ref, v_rows, bias_pair in srcs:
            k = k_ref[k_rows, kcols].astype(BF16)
            v = v_ref[v_rows, vcols].astype(BF16)
            sources.append((k, v, None if bias_pair is None else bias_pair[hh]))
        outs.append(_attend(q, sources))
    return jnp.where(low, outs[0], outs[1])


def _attn_kernel(*refs, lq, dk, n_pairs, n_src):
    q_ref = refs[0]
    kv = refs[1:1 + 2 * n_src]
    o_ref = refs[1 + 2 * n_src]
    full = slice(None)
    for c in range(lq // Q_CHUNK):
        rows = slice(c * Q_CHUNK, (c + 1) * Q_CHUNK)
        for pair in range(n_pairs):
            srcs = [(kv[2 * i], full, kv[2 * i + 1], full, None) for i in range(n_src)]
            o = _pair_outputs(q_ref, rows, pair, dk, srcs, Q_CHUNK)
            o_ref[rows, pair * LANES:(pair + 1) * LANES] = o.astype(o_ref.dtype)


def _na_kernel(q_ref, k_ref, v_ref, ck_ref, cv_ref, pb_ref, o_ref, bias_ref):
    @pl.when(pl.program_id(1) == 0)
    def _():
        low = lax.broadcasted_iota(jnp.int32, (GRID_W, LANES), 1) < GRID_W
        neg = jnp.full((GRID_W, LANES), NEG_INF, F32)
        for hh in range(2):
            for r in range(GRID_R):
                win0 = (r // 8) * 4
                rs = min(max(r - KH // 2, 0), GRID_R - KH)
                delta, dr0 = rs - win0, rs - r + KH - 1
                for p in range(NA_WIN_ROWS // 2):
                    ok0 = delta <= 2 * p < delta + KH
                    ok1 = delta <= 2 * p + 1 < delta + KH
                    if not (ok0 or ok1):
                        blk = neg
                    else:
                        blk = pb_ref[hh, dr0 + 2 * p - delta + 1]
                        if not ok0:
                            blk = jnp.where(low, NEG_INF, blk)
                        if not ok1:
                            blk = jnp.where(low, blk, NEG_INF)
                    bias_ref[hh, r * GRID_W:(r + 1) * GRID_W, p * LANES:(p + 1) * LANES] = blk

    full = slice(None)
    for c in range(GRID_R * GRID_W // Q_CHUNK):
        rows = slice(c * Q_CHUNK, (c + 1) * Q_CHUNK)
        row_lo = min(max(4 * c - KH // 2, 0), GRID_R - KH)
        row_hi = min(max(4 * c + 3 - KH // 2, 0), GRID_R - KH) + KH
        row_hi += (row_hi - row_lo) % 2
        win = slice(row_lo * GRID_W, row_hi * GRID_W)
        strip = slice((row_lo - (c // 2) * 4) * GRID_W, (row_hi - (c // 2) * 4) * GRID_W)
        bias_pair = (bias_ref[0, rows, strip], bias_ref[1, rows, strip])
        srcs = [(k_ref, win, v_ref, win, bias_pair), (ck_ref, full, cv_ref, full, None)]
        o = _pair_outputs(q_ref, rows, 0, NA_DH, srcs, Q_CHUNK)
        o_ref[rows, :] = o.astype(o_ref.dtype)


def _out_mlp_kernel(y_ref, b1_ref, b2_ref, mod_ref, wo_ref, g2_ref, w1_ref, w2_ref, o_ref, *, hc):
    mod = mod_ref[...]
    mix = _dot(b1_ref[...], wo_ref[0:A_W, :]) + _dot(b2_ref[...], wo_ref[A_W:2 * A_W, :])
    y1 = y_ref[...] + mod[:, 2 * D:3 * D] * mix
    h2 = _modulated(y1, g2_ref[...], mod[:, 3 * D:4 * D], mod[:, 4 * D:5 * D]).astype(BF16)
    acc = jnp.zeros(y1.shape, F32)
    for c in range(MLP_HIDDEN // hc):
        a = jnp.maximum(_dot(h2, w1_ref[:, c * hc:(c + 1) * hc]), 0.0)
        acc = acc + _dot((a * a).astype(BF16), w2_ref[c * hc:(c + 1) * hc, :])
    o_ref[...] = y1 + mod[:, 5 * D:6 * D] * acc


def _params(n_grid):
    return pltpu.CompilerParams(dimension_semantics=("arbitrary",) * n_grid,
                                vmem_limit_bytes=VMEM_LIMIT)


def _layer_spec(shape, layer):
    nd = len(shape)
    return pl.BlockSpec((None,) + tuple(shape[1:]), lambda *_: (layer,) + (0,) * (nd - 1))


def _mod_spec(layer, row_of_batch):
    return pl.BlockSpec((None, None, 1, 6 * D), lambda b, *_: (layer, row_of_batch(b), 0, 0))


def _ada_call(c_all, ada_w, ada_b):
    tn = 1536
    return pl.pallas_call(
        _ada_kernel,
        out_shape=jax.ShapeDtypeStruct((DEPTH, MOD_ROWS, 6 * D), F32),
        grid=(DEPTH, 6 * D // tn),
        in_specs=[pl.BlockSpec((MOD_ROWS, D), lambda l, j: (0, 0)),
                  pl.BlockSpec((None, D, tn), lambda l, j: (l, 0, j)),
                  pl.BlockSpec((None, 1, tn), lambda l, j: (l, 0, j))],
        out_specs=pl.BlockSpec((None, MOD_ROWS, tn), lambda l, j: (l, 0, j)),
        compiler_params=_params(2), name="ada_mod",
    )(c_all, ada_w, ada_b)


def _even_in_call(y, mods, layer, i, row_of_batch, pp, kv_dtype):
    nb, seq, _ = y.shape
    tok = lambda w: pl.BlockSpec((None, seq, w), lambda b: (b, 0, 0))
    return pl.pallas_call(
        functools.partial(_even_in_kernel, seq=seq),
        out_shape=(jax.ShapeDtypeStruct((nb, seq, A_W), BF16),
                   jax.ShapeDtypeStruct((nb, seq, A_W), BF16),
                   jax.ShapeDtypeStruct((nb, seq, A_W), kv_dtype),
                   jax.ShapeDtypeStruct((nb, seq, A_W), kv_dtype)),
        grid=(nb,),
        in_specs=[tok(D), _mod_spec(layer, row_of_batch), _layer_spec(pp["norm1_g"].shape, layer),
                  _layer_spec(pp["even_w_in"].shape, i), _layer_spec(pp["even_conv_w"].shape, i),
                  _layer_spec(pp["na_q_norm"].shape, i), _layer_spec(pp["na_k_norm"].shape, i)],
        out_specs=(tok(A_W), tok(A_W), tok(A_W), tok(A_W)),
        compiler_params=_params(1), name="even_in",
    )(y, mods, pp["norm1_g"], pp["even_w_in"], pp["even_conv_w"], pp["na_q_norm"], pp["na_k_norm"])


def _odd_in_call(y, mods, layer, i, row_of_batch, pp, rope_tabs, states):
    nb, seq, _ = y.shape
    tok = lambda w: pl.BlockSpec((None, seq, w), lambda b: (b, 0, 0))
    rope = rope_tabs is not None
    names = ["norm1_g", "odd_w_in", "pool_w", "pool_scale", "q_a_norm",
             "w_q_b_rope" if rope else "w_q_b", "kv_a_norm", "w_kv_b", "mla_q_norm", "mla_k_norm"]
    layers = [layer] + [i] * 9
    in_specs = [tok(D), _mod_spec(layer, row_of_batch)]
    in_specs += [_layer_spec(pp[n].shape, li) for n, li in zip(names, layers)]
    args = [y, mods] + [pp[n] for n in names]
    if rope:
        in_specs += [pl.BlockSpec((seq, MLA_HP), lambda b: (0, 0))] * 2
        in_specs += [_layer_spec(pp["mla_q_norm_partner"].shape, i)]
        args += list(rope_tabs) + [pp["mla_q_norm_partner"]]
    out_shape = [jax.ShapeDtypeStruct((nb, seq, A_W), BF16),
                 jax.ShapeDtypeStruct((nb, seq, MLA_HEADS * MLA_HP), BF16),
                 jax.ShapeDtypeStruct((nb, seq, MLA_HEADS * MLA_HP), BF16),
                 jax.ShapeDtypeStruct((nb, seq, MLA_HEADS * V_DIM), BF16)]
    out_specs = [tok(A_W), tok(MLA_HEADS * MLA_HP), tok(MLA_HEADS * MLA_HP), tok(MLA_HEADS * V_DIM)]
    if states:
        out_shape += [jax.ShapeDtypeStruct((nb, seq, KV_LORA), F32),
                      jax.ShapeDtypeStruct((nb, seq, MLA_HP), F32)]
        out_specs += [tok(KV_LORA), tok(MLA_HP)]
    return pl.pallas_call(
        functools.partial(_odd_in_kernel, seq=seq, rope=rope_tabs is not None, states=states),
        out_shape=tuple(out_shape), grid=(nb,), in_specs=in_specs, out_specs=tuple(out_specs),
        compiler_params=_params(1), name="odd_in",
    )(*args)


def _cache_kv_call(cache_ckv, cache_kpe_slot, i, pp):
    nb, _, past, _ = cache_ckv.shape
    cache = lambda w: pl.BlockSpec((None, None, past, w), lambda b: (b, i, 0, 0))
    tok = lambda w: pl.BlockSpec((None, past, w), lambda b: (b, 0, 0))
    return pl.pallas_call(
        _cache_kv_kernel,
        out_shape=(jax.ShapeDtypeStruct((nb, past, MLA_HEADS * MLA_HP), BF16),
                   jax.ShapeDtypeStruct((nb, past, MLA_HEADS * V_DIM), BF16)),
        grid=(nb,),
        in_specs=[cache(KV_LORA), cache(MLA_HP), _layer_spec(pp["w_kv_b"].shape, i),
                  _layer_spec(pp["mla_k_norm"].shape, i)],
        out_specs=(tok(MLA_HEADS * MLA_HP), tok(MLA_HEADS * V_DIM)),
        compiler_params=_params(1), name="cache_kv",
    )(cache_ckv, cache_kpe_slot, pp["w_kv_b"], pp["mla_k_norm"])


def _attn_call(q, kvs, dk, pairs_per_step):
    nb, lq, _ = q.shape
    n_groups = NA_HEADS // (2 * pairs_per_step)
    qk_w, v_w = 2 * dk * pairs_per_step, LANES * pairs_per_step
    in_specs = [pl.BlockSpec((None, lq, qk_w), lambda g, b: (b, 0, g))]
    args = [q]
    for k, v in kvs:
        in_specs += [pl.BlockSpec((None, k.shape[1], qk_w), lambda g, b: (b, 0, g)),
                     pl.BlockSpec((None, v.shape[1], v_w), lambda g, b: (b, 0, g))]
        args += [k, v]
    return pl.pallas_call(
        functools.partial(_attn_kernel, lq=lq, dk=dk, n_pairs=pairs_per_step, n_src=len(kvs)),
        out_shape=jax.ShapeDtypeStruct((nb, lq, NA_HEADS * V_DIM), BF16),
        grid=(n_groups, nb), in_specs=in_specs,
        out_specs=pl.BlockSpec((None, lq, v_w), lambda g, b: (b, 0, g)),
        compiler_params=_params(2), name="attn",
    )(*args)


def _na_call(q, k, v, cache_k, cache_v, i, pair_bias):
    nb, lq, _ = q.shape
    past = cache_k.shape[2]
    tok = pl.BlockSpec((None, lq, LANES), lambda g, b: (b, 0, g))
    cache = pl.BlockSpec((None, None, past, LANES), lambda g, b: (b, i, 0, g))
    return pl.pallas_call(
        _na_kernel,
        out_shape=jax.ShapeDtypeStruct((nb, lq, NA_HEADS * NA_DH), BF16),
        grid=(NA_HEADS // 2, nb),
        in_specs=[tok, tok, tok, cache, cache,
                  pl.BlockSpec((None, 2, 2 * KH, GRID_W, LANES), lambda g, b: (i, g, 0, 0, 0))],
        out_specs=tok,
        scratch_shapes=[pltpu.VMEM((2, lq, NA_WIN_ROWS * GRID_W), F32)],
        compiler_params=_params(2), name="na_attn",
    )(q, k, v, cache_k, cache_v, pair_bias)


def _out_mlp_call(y, b1, b2, mods, layer, row_of_batch, w_out, w_out_layer, pp):
    nb, seq, _ = y.shape
    tm = 512
    tok = lambda w: pl.BlockSpec((None, tm, w), lambda b, t: (b, t, 0))
    return pl.pallas_call(
        functools.partial(_out_mlp_kernel, hc=512),
        out_shape=jax.ShapeDtypeStruct(y.shape, F32),
        grid=(nb, seq // tm),
        in_specs=[tok(D), tok(A_W), tok(A_W), _mod_spec(layer, row_of_batch),
                  _layer_spec(w_out.shape, w_out_layer), _layer_spec(pp["norm2_g"].shape, layer),
                  _layer_spec(pp["mlp_w1"].shape, layer), _layer_spec(pp["mlp_w2"].shape, layer)],
        out_specs=tok(D),
        compiler_params=_params(2), name="out_mlp",
    )(y, b1, b2, mods, w_out, pp["norm2_g"], pp["mlp_w1"], pp["mlp_w2"])


def _row(v):
    return v[:, None, :].astype(F32)


def _head_slot(v):
    return _row(jnp.pad(v, ((0, 0), (0, MLA_HP - QK_DIM))))


def _swap_rotary_halves(v):
    half = ROPE // 2
    return jnp.concatenate(
        [jnp.zeros_like(v[..., :NOPE]), v[..., NOPE + half:], v[..., NOPE:NOPE + half]], axis=-1)


def _prepare(p):
    n_odd = p["odd_w_in"].shape[0]
    pp = {}
    pp["norm1_g"], pp["norm2_g"] = _row(p["norm1_g"]), _row(p["norm2_g"])
    pp["mlp_w1"], pp["mlp_w2"] = p["mlp_w1"].astype(BF16), p["mlp_w2"].astype(BF16)
    pp["even_w_in"] = p["even_w_in"].astype(BF16)
    pp["even_conv_w"] = p["even_conv_w"].astype(F32)
    pp["na_q_norm"] = _row(jnp.tile(p["na_q_norm"], (1, NA_HEADS)))
    pp["na_k_norm"] = _row(jnp.tile(p["na_k_norm"], (1, NA_HEADS)))
    pp["even_w_out"] = p["even_w_out"].astype(BF16)
    pp["odd_w_out"] = p["odd_w_out"].astype(BF16)
    w = p["odd_w_in"]
    o_pe = A_W + Q_LORA + KV_LORA
    pe = jnp.pad(w[:, :, o_pe:], ((0, 0), (0, 0), (NOPE, MLA_HP - QK_DIM)))
    pp["odd_w_in"] = jnp.concatenate([w[:, :, :o_pe], pe], axis=-1).astype(BF16)
    pp["pool_w"] = p["pool_w"].astype(BF16)
    pp["pool_scale"] = _row(p["pool_scale"])
    pp["q_a_norm"], pp["kv_a_norm"] = _row(p["q_a_norm"]), _row(p["kv_a_norm"])
    slot_pad = ((0, 0), (0, 0), (0, 0), (0, MLA_HP - QK_DIM))
    wq = p["w_q_b"].reshape(n_odd, Q_LORA, MLA_HEADS, QK_DIM)
    wq_slots = jnp.pad(wq, slot_pad).reshape(n_odd, Q_LORA, MLA_HEADS * MLA_HP)
    wq_partner = jnp.pad(_swap_rotary_halves(wq), slot_pad).reshape(n_odd, Q_LORA, MLA_HEADS * MLA_HP)
    pp["w_q_b"] = wq_slots.astype(BF16)
    pp["w_q_b_rope"] = jnp.concatenate([wq_slots, wq_partner], axis=-1).astype(BF16)
    wkv = p["w_kv_b"].reshape(n_odd, KV_LORA, MLA_HEADS, NOPE + V_DIM)
    wk = jnp.pad(wkv[..., :NOPE], ((0, 0), (0, 0), (0, 0), (0, MLA_HP - NOPE)))
    pp["w_kv_b"] = jnp.concatenate(
        [wk.reshape(n_odd, KV_LORA, MLA_HEADS * MLA_HP),
         wkv[..., NOPE:].reshape(n_odd, KV_LORA, MLA_HEADS * V_DIM)], axis=-1).astype(BF16)
    pp["mla_q_norm"], pp["mla_k_norm"] = _head_slot(p["mla_q_norm"]), _head_slot(p["mla_k_norm"])
    pp["mla_q_norm_partner"] = _head_slot(_swap_rotary_halves(p["mla_q_norm"]))
    return pp


def _rope_tables(n):
    t = jnp.arange(n)
    row = (t // GRID_W).astype(F32)
    col = (t % GRID_W).astype(F32)
    axis_dim = ROPE // 2
    inv = 1.0 / (ROPE_BASE ** (jnp.arange(0, axis_dim, 2, dtype=F32) / axis_dim))
    ang = jnp.concatenate([row[:, None] * inv, col[:, None] * inv], axis=-1)
    cos, sin = jnp.cos(ang), jnp.sin(ang)
    tail = jnp.zeros((n, MLA_HP - QK_DIM), F32)
    rc = jnp.concatenate([jnp.ones((n, NOPE), F32), cos, cos, tail], axis=-1)
    rs = jnp.concatenate([jnp.zeros((n, NOPE), F32), -sin, sin, tail], axis=-1)
    return rc, rs


def _na_pair_bias(rpb):
    c = np.arange(GRID_W)
    start = np.clip(c - KW // 2, 0, GRID_W - KW)
    kc = np.arange(GRID_W)
    valid = (kc[None, :] >= start[:, None]) & (kc[None, :] < start[:, None] + KW)
    dc = kc[None, :] - c[:, None] + KW - 1
    onehot = ((dc[None] == np.arange(2 * KW - 1)[:, None, None]) & valid[None]).astype(np.float32)
    blocks = jnp.einsum("lhij,jck->lhick", rpb.astype(F32), onehot, precision=lax.Precision.HIGHEST)
    blocks = jnp.where(valid, blocks, NEG_INF)
    blocks = jnp.pad(blocks, ((0, 0), (0, 0), (1, 1), (0, 0), (0, 0)), constant_values=NEG_INF)
    return jnp.concatenate([blocks[:, :, :-1], blocks[:, :, 1:]], axis=-1).astype(F32)


def kernel(x_prompt, x_sample, cache_na_k, cache_na_v, cache_mla_ckv, cache_mla_kpe, c, c_ctx,
           ada_w, ada_b, norm1_g, norm2_g, mlp_w1, mlp_w2,
           even_w_in, even_conv_w, na_q_norm, na_k_norm, na_rpb, even_w_out,
           odd_w_in, pool_w, pool_scale, q_a_norm, w_q_b, kv_a_norm, w_kv_b,
           mla_q_norm, mla_k_norm, odd_w_out):
    bp, lp, _ = x_prompt.shape
    bs, ls, _ = x_sample.shape
    pp = _prepare(dict(norm1_g=norm1_g, norm2_g=norm2_g, mlp_w1=mlp_w1, mlp_w2=mlp_w2,
                       even_w_in=even_w_in, even_conv_w=even_conv_w, na_q_norm=na_q_norm,
                       na_k_norm=na_k_norm, even_w_out=even_w_out, odd_w_in=odd_w_in, pool_w=pool_w,
                       pool_scale=pool_scale, q_a_norm=q_a_norm, w_q_b=w_q_b, kv_a_norm=kv_a_norm,
                       w_kv_b=w_kv_b, mla_q_norm=mla_q_norm, mla_k_norm=mla_k_norm,
                       odd_w_out=odd_w_out))
    c_all = jnp.concatenate(
        [c_ctx[None, :], c, jnp.zeros((MOD_ROWS - 1 - bs, D), F32)], axis=0)
    mods = _ada_call(c_all, ada_w, ada_b[:, None, :]).reshape(DEPTH, MOD_ROWS, 1, 6 * D)
    rope_tabs = _rope_tables(ls)
    pair_bias = _na_pair_bias(na_rpb)
    past = cache_na_k.shape[2]
    cache_k = cache_na_k.reshape(bs, -1, past, NA_HEADS * NA_DH)
    cache_v = cache_na_v.reshape(bs, -1, past, NA_HEADS * NA_DH)
    cache_kpe_slot = jnp.pad(cache_mla_kpe, ((0, 0), (0, 0), (0, 0), (NOPE, MLA_HP - QK_DIM)))
    ctx_row = lambda b: 0
    lat_row = lambda b: b + 1

    yp, ys = x_prompt, x_sample
    na_k, na_v, mla_ckv, mla_kpe = [], [], [], []
    for l in range(DEPTH):
        i = l // 2
        if l % 2 == 0:
            ap, qp, kp, vp = _even_in_call(yp, mods, l, i, ctx_row, pp, F32)
            att_p = _attn_call(qp, [(kp, vp)], NA_DH, 4)
            na_k.append(kp.reshape(bp, lp, NA_HEADS, NA_DH))
            na_v.append(vp.reshape(bp, lp, NA_HEADS, NA_DH))
            a_s, qs, k_s, vs = _even_in_call(ys, mods, l, i, lat_row, pp, BF16)
            att_s = _na_call(qs, k_s, vs, cache_k, cache_v, i, pair_bias)
            w_out = pp["even_w_out"]
        else:
            ap, qp, kp, vp, ckvp, kpep = _odd_in_call(yp, mods, l, i, ctx_row, pp, None, True)
            att_p = _attn_call(qp, [(kp, vp)], MLA_HP, 4)
            mla_ckv.append(ckvp)
            mla_kpe.append(kpep[:, :, NOPE:QK_DIM])
            a_s, qs, k_s, vs = _odd_in_call(ys, mods, l, i, lat_row, pp, rope_tabs, False)
            kc, vc = _cache_kv_call(cache_mla_ckv, cache_kpe_slot, i, pp)
            att_s = _attn_call(qs, [(kc, vc), (k_s, vs)], MLA_HP, 1)
            w_out = pp["odd_w_out"]
        yp = _out_mlp_call(yp.reshape(1, bp * lp, D), ap.reshape(1, bp * lp, A_W),
                           att_p.reshape(1, bp * lp, A_W), mods, l, ctx_row, w_out, i,
                           pp).reshape(bp, lp, D)
        ys = _out_mlp_call(ys, a_s, att_s, mods, l, lat_row, w_out, i, pp)
    return (yp, ys, jnp.stack(na_k, axis=1), jnp.stack(na_v, axis=1),
            jnp.stack(mla_ckv, axis=1), jnp.stack(mla_kpe, axis=1))
```

```python
import functools

import jax
import jax.numpy as jnp
import numpy as np
from jax import lax
from jax.experimental import pallas as pl
from jax.experimental.pallas import tpu as pltpu

F32 = jnp.float32
BF16 = jnp.bfloat16

D = 1024
DEPTH = 4
GRID_W = 64
GRID_R = 16
A_W = 512
NA_HEADS = 8
NA_DH = 64
KH = 8
KW = 16
POOL_WINDOWS = (2, 4, 8, 16)
POOL_G = 128
POOL_PAD = 16
MLA_HEADS = 8
NOPE = 64
ROPE = 32
QK_DIM = NOPE + ROPE
MLA_HP = 128
V_DIM = 64
Q_LORA = 384
KV_LORA = 256
ROPE_BASE = 10000.0
MLP_HIDDEN = 4 * D
NEG_INF = -1e30
EPS = 1e-6
MOD_ROWS = 16
LANES = 128
NA_WIN_ROWS = 12
Q_CHUNK = 256
VMEM_LIMIT = 56 * 1024 * 1024


def _dot(a, b):
    return jnp.dot(a, b, preferred_element_type=F32)


def _dot_nt(a, b):
    return lax.dot_general(a, b, (((1,), (1,)), ((), ())), preferred_element_type=F32)


def _rms(x, g):
    ms = jnp.mean(x * x, axis=-1, keepdims=True)
    return x * lax.rsqrt(ms + EPS) * g


def _modulated(x, g, shift, scale):
    return _rms(x, g) * (1.0 + scale) + shift


def _head_rms_64(x, g):
    t_rows = x.shape[0]
    low = lax.broadcasted_iota(jnp.int32, (t_rows, LANES), 1) < NA_DH
    cols = []
    for j in range(x.shape[1] // LANES):
        t = x[:, j * LANES:(j + 1) * LANES]
        t2 = t * t
        s_lo = jnp.sum(jnp.where(low, t2, 0.0), axis=-1, keepdims=True)
        s_hi = jnp.sum(jnp.where(low, 0.0, t2), axis=-1, keepdims=True)
        r_lo = lax.rsqrt(s_lo * (1.0 / NA_DH) + EPS)
        r_hi = lax.rsqrt(s_hi * (1.0 / NA_DH) + EPS)
        cols.append(t * jnp.where(low, r_lo, r_hi))
    return jnp.concatenate(cols, axis=-1) * g


def _slot_inv_rms(t):
    return lax.rsqrt(jnp.sum(t * t, axis=-1, keepdims=True) * (1.0 / QK_DIM) + EPS)


def _rotary_partner(t):
    x1_side = lax.broadcasted_iota(jnp.int32, t.shape, 1) < NOPE + ROPE // 2
    return jnp.where(x1_side, pltpu.roll(t, LANES - ROPE // 2, 1), pltpu.roll(t, ROPE // 2, 1))


def _mla_q(qf, q_partner, g_slot, g_partner, rope_tabs):
    if rope_tabs is not None:
        rc, rs = rope_tabs
        gc, gs = g_slot * rc, g_partner * rs
    cols = []
    for j in range(MLA_HEADS):
        t = qf[:, j * MLA_HP:(j + 1) * MLA_HP]
        if rope_tabs is None:
            u = t * g_slot
        else:
            u = t * gc + q_partner[:, j * MLA_HP:(j + 1) * MLA_HP] * gs
        cols.append(u * (_slot_inv_rms(t) * (QK_DIM ** -0.5)))
    return jnp.concatenate(cols, axis=-1)


def _kv_up(ckv, kpe_slot, wkvb_ref, g_slot, rope_tabs):
    kvb = _dot(ckv.astype(BF16), wkvb_ref[...])
    kr = kpe_slot * g_slot
    if rope_tabs is not None:
        rc, rs = rope_tabs
        kr = kr * rc + _rotary_partner(kr) * rs
    cols = []
    for j in range(MLA_HEADS):
        kn = kvb[:, j * MLA_HP:(j + 1) * MLA_HP]
        cols.append((kn * g_slot + kr) * _slot_inv_rms(kn + kpe_slot))
    return jnp.concatenate(cols, axis=-1), kvb[:, MLA_HEADS * MLA_HP:]


def _ada_kernel(c_ref, w_ref, b_ref, o_ref):
    c = c_ref[...]
    s = c * (1.0 / (1.0 + jnp.exp(-c)))
    o_ref[...] = _dot(s.astype(BF16), w_ref[...].astype(BF16)) + b_ref[...]


def _even_in_kernel(y_ref, mod_ref, g1_ref, w_ref, cw_ref, gq_ref, gk_ref, *out_refs, seq, context):
    if context:
        a_ref, att_ref, k_ref, v_ref, q_ref = out_refs
    else:
        a_ref, q_ref, k_ref, v_ref = out_refs
    mod = mod_ref[...]
    h = _modulated(y_ref[...], g1_ref[...], mod[:, 0:D], mod[:, D:2 * D])
    z = _dot(h.astype(BF16), w_ref[...])
    u = z[:, A_W:2 * A_W] * z[:, 2 * A_W:3 * A_W]
    t = lax.broadcasted_iota(jnp.int32, (seq, A_W), 0)
    prev = jnp.where(t > 0, pltpu.roll(u, 1, 0), 0.0)
    nxt = jnp.where(t < seq - 1, pltpu.roll(u, seq - 1, 0), 0.0)
    cw = cw_ref[...]
    a = z[:, 0:A_W] * (prev * cw[0:1] + u * cw[1:2] + nxt * cw[2:3])
    a_ref[...] = a.astype(BF16)
    q = _head_rms_64(z[:, 3 * A_W:4 * A_W], gq_ref[...]) * (NA_DH ** -0.5)
    q_ref[...] = q.astype(BF16)
    k_ref[...] = _head_rms_64(z[:, 4 * A_W:5 * A_W], gk_ref[...]).astype(k_ref.dtype)
    v_ref[...] = z[:, 5 * A_W:6 * A_W].astype(v_ref.dtype)
    if context:
        _attend_all(q_ref, [k_ref, v_ref], att_ref, seq, NA_DH, NA_HEADS // 2)


def _pool_branch(u, pw_ref, ps, seq):
    zpad = jnp.zeros((POOL_PAD, POOL_G), F32)
    padded = seq + 2 * POOL_PAD
    t = lax.broadcasted_iota(jnp.int32, (seq, POOL_G), 0)
    cols = []
    for g, w in enumerate(POOL_WINDOWS):
        ug = u[:, g * POOL_G:(g + 1) * POOL_G]
        xp = jnp.concatenate([zpad, ug, zpad], axis=0)
        acc = xp + pltpu.roll(xp, 1, 0)
        s = 1
        while 4 * s <= w:
            acc = pltpu.roll(acc, s, 0) + pltpu.roll(acc, padded - s, 0)
            s *= 2
        win = acc[POOL_PAD:POOL_PAD + seq]
        lo = jnp.maximum(t - w // 2, 0)
        hi = jnp.minimum(t - w // 2 + w, seq)
        pooled = win / (hi - lo).astype(F32)
        cols.append(_dot((pooled - ug).astype(BF16), pw_ref[g]))
    return jnp.concatenate(cols, axis=-1) * ps


def _odd_in_kernel(*refs, seq, rope, context):
    (y_ref, mod_ref, g1_ref, w_ref, pw_ref, ps_ref, qan_ref, wqb_ref, kvan_ref, wkvb_ref,
     gq_ref, gk_ref) = refs[:12]
    pos = 12
    rope_tabs = g_partner = q_partner = None
    if rope:
        rope_tabs = (refs[pos][...], refs[pos + 1][...])
        g_partner = refs[pos + 2][...]
        pos += 3
    if context:
        p_ref, att_ref, ckv_ref, kpe_ref, q_ref, k_ref, v_ref = refs[pos:pos + 7]
    else:
        p_ref, q_ref, k_ref, v_ref = refs[pos:pos + 4]
    mod = mod_ref[...]
    h = _modulated(y_ref[...], g1_ref[...], mod[:, 0:D], mod[:, D:2 * D])
    z = _dot(h.astype(BF16), w_ref[...])
    p_ref[...] = _pool_branch(z[:, 0:A_W], pw_ref, ps_ref[...], seq).astype(BF16)
    o_q, o_kv, o_pe = A_W, A_W + Q_LORA, A_W + Q_LORA + KV_LORA
    ql = _rms(z[:, o_q:o_kv], qan_ref[...])
    qf = _dot(ql.astype(BF16), wqb_ref[...])
    if rope:
        q_partner = qf[:, MLA_HEADS * MLA_HP:]
    q_ref[...] = _mla_q(qf, q_partner, gq_ref[...], g_partner, rope_tabs).astype(BF16)
    ckv = _rms(z[:, o_kv:o_pe], kvan_ref[...])
    kpe_slot = z[:, o_pe:o_pe + MLA_HP]
    k, v = _kv_up(ckv, kpe_slot, wkvb_ref, gk_ref[...], rope_tabs)
    k_ref[...] = k.astype(BF16)
    v_ref[...] = v.astype(BF16)
    if context:
        ckv_ref[...] = ckv
        kpe_ref[...] = kpe_slot
        _attend_all(q_ref, [k_ref, v_ref], att_ref, seq, MLA_HP, MLA_HEADS // 2)


def _cache_kv_kernel(ckv_ref, kpe_ref, wkvb_ref, gk_ref, k_ref, v_ref):
    k, v = _kv_up(ckv_ref[...], kpe_ref[...], wkvb_ref, gk_ref[...], None)
    k_ref[...] = k.astype(BF16)
    v_ref[...] = v.astype(BF16)


def _attend(q, sources):
    scores = []
    for k, _, bias in sources:
        s = _dot_nt(q, k)
        scores.append(s if bias is None else s + bias)
    m = scores[0].max(axis=-1, keepdims=True)
    for s in scores[1:]:
        m = jnp.maximum(m, s.max(axis=-1, keepdims=True))
    den = None
    out = None
    for s, (_, v, _) in zip(scores, sources):
        p = jnp.exp(s - m)
        ps = p.sum(axis=-1, keepdims=True)
        pv = _dot(p.astype(BF16), v)
        den = ps if den is None else den + ps
        out = pv if out is None else out + pv
    return out / den


def _pair_outputs(q_ref, q_rows, pair, dk, srcs, n_rows):
    low = lax.broadcasted_iota(jnp.int32, (n_rows, LANES), 1) < V_DIM
    vcols = slice(pair * LANES, (pair + 1) * LANES)
    outs = []
    for hh in range(2):
        if dk == NA_DH:
            q128 = q_ref[q_rows, pair * LANES:(pair + 1) * LANES]
            keep = low if hh == 0 else jnp.logical_not(low)
            q = jnp.where(keep, q128, jnp.zeros_like(q128))
            kcols = slice(pair * LANES, (pair + 1) * LANES)
        else:
            kcols = slice((2 * pair + hh) * dk, (2 * pair + hh + 1) * dk)
            q = q_ref[q_rows, kcols]
        sources = []
        for k_ref, k_rows, v_ref, v_rows, bias_pair in srcs:
            k = k_ref[k_rows, kcols].astype(BF16)
            v = v_ref[v_rows, vcols].astype(BF16)
            sources.append((k, v, None if bias_pair is None else bias_pair[hh]))
        outs.append(_attend(q, sources))
    return jnp.where(low, outs[0], outs[1])


def _attend_all(q_ref, kv, o_ref, lq, dk, n_pairs):
    full = slice(None)
    for c in range(lq // Q_CHUNK):
        rows = slice(c * Q_CHUNK, (c + 1) * Q_CHUNK)
        for pair in range(n_pairs):
            srcs = [(kv[2 * i], full, kv[2 * i + 1], full, None) for i in range(len(kv) // 2)]
            o = _pair_outputs(q_ref, rows, pair, dk, srcs, Q_CHUNK)
            o_ref[rows, pair * LANES:(pair + 1) * LANES] = o.astype(o_ref.dtype)


def _attn_kernel(*refs, lq, dk, n_pairs, n_src):
    _attend_all(refs[0], refs[1:1 + 2 * n_src], refs[1 + 2 * n_src], lq, dk, n_pairs)


def _na_kernel(q_ref, k_ref, v_ref, ck_ref, cv_ref, pb_ref, o_ref, bias_ref):
    @pl.when(pl.program_id(1) == 0)
    def _():
        low = lax.broadcasted_iota(jnp.int32, (GRID_W, LANES), 1) < GRID_W
        neg = jnp.full((GRID_W, LANES), NEG_INF, F32)
        for hh in range(2):
            for r in range(GRID_R):
                win0 = (r // 8) * 4
                rs = min(max(r - KH // 2, 0), GRID_R - KH)
                delta, dr0 = rs - win0, rs - r + KH - 1
                for p in range(NA_WIN_ROWS // 2):
                    ok0 = delta <= 2 * p < delta + KH
                    ok1 = delta <= 2 * p + 1 < delta + KH
                    if not (ok0 or ok1):
                        blk = neg
                    else:
                        blk = pb_ref[hh, dr0 + 2 * p - delta + 1]
                        if not ok0:
                            blk = jnp.where(low, NEG_INF, blk)
                        if not ok1:
                            blk = jnp.where(low, blk, NEG_INF)
                    bias_ref[hh, r * GRID_W:(r + 1) * GRID_W, p * LANES:(p + 1) * LANES] = blk

    full = slice(None)
    for c in range(GRID_R * GRID_W // Q_CHUNK):
        rows = slice(c * Q_CHUNK, (c + 1) * Q_CHUNK)
        row_lo = min(max(4 * c - KH // 2, 0), GRID_R - KH)
        row_hi = min(max(4 * c + 3 - KH // 2, 0), GRID_R - KH) + KH
        row_hi += (row_hi - row_lo) % 2
        win = slice(row_lo * GRID_W, row_hi * GRID_W)
        strip = slice((row_lo - (c // 2) * 4) * GRID_W, (row_hi - (c // 2) * 4) * GRID_W)
        bias_pair = (bias_ref[0, rows, strip], bias_ref[1, rows, strip])
        srcs = [(k_ref, win, v_ref, win, bias_pair), (ck_ref, full, cv_ref, full, None)]
        o = _pair_outputs(q_ref, rows, 0, NA_DH, srcs, Q_CHUNK)
        o_ref[rows, :] = o.astype(o_ref.dtype)


def _out_mlp_kernel(y_ref, b1_ref, b2_ref, mod_ref, wo_ref, g2_ref, w1_ref, w2_ref, o_ref, *, hc):
    mod = mod_ref[...]
    mix = _dot(b1_ref[...], wo_ref[0:A_W, :]) + _dot(b2_ref[...], wo_ref[A_W:2 * A_W, :])
    y1 = y_ref[...] + mod[:, 2 * D:3 * D] * mix
    h2 = _modulated(y1, g2_ref[...], mod[:, 3 * D:4 * D], mod[:, 4 * D:5 * D]).astype(BF16)
    acc = jnp.zeros(y1.shape, F32)
    for c in range(MLP_HIDDEN // hc):
        a = jnp.maximum(_dot(h2, w1_ref[:, c * hc:(c + 1) * hc]), 0.0)
        acc = acc + _dot((a * a).astype(BF16), w2_ref[c * hc:(c + 1) * hc, :])
    o_ref[...] = y1 + mod[:, 5 * D:6 * D] * acc


def _params(n_grid):
    return pltpu.CompilerParams(dimension_semantics=("arbitrary",) * n_grid,
                                vmem_limit_bytes=VMEM_LIMIT)


def _layer_spec(shape, layer):
    nd = len(shape)
    return pl.BlockSpec((None,) + tuple(shape[1:]), lambda *_: (layer,) + (0,) * (nd - 1),
                        pipeline_mode=pl.Buffered(1))


def _mod_spec(layer, row_of_batch):
    return pl.BlockSpec((None, None, 1, 6 * D), lambda b, *_: (layer, row_of_batch(b), 0, 0))


def _ada_call(c_all, ada_w, ada_b):
    tn = 1536
    return pl.pallas_call(
        _ada_kernel,
        out_shape=jax.ShapeDtypeStruct((DEPTH, MOD_ROWS, 6 * D), F32),
        grid=(DEPTH, 6 * D // tn),
        in_specs=[pl.BlockSpec((MOD_ROWS, D), lambda l, j: (0, 0)),
                  pl.BlockSpec((None, D, tn), lambda l, j: (l, 0, j)),
                  pl.BlockSpec((None, 1, tn), lambda l, j: (l, 0, j))],
        out_specs=pl.BlockSpec((None, MOD_ROWS, tn), lambda l, j: (l, 0, j)),
        compiler_params=_params(2), name="ada_mod",
    )(c_all, ada_w, ada_b)


def _even_in_call(y, mods, layer, i, row_of_batch, pp, context):
    nb, seq, _ = y.shape
    tok = lambda w: pl.BlockSpec((None, seq, w), lambda b: (b, 0, 0))
    kv_dtype = F32 if context else BF16
    return pl.pallas_call(
        functools.partial(_even_in_kernel, seq=seq, context=context),
        out_shape=(jax.ShapeDtypeStruct((nb, seq, A_W), BF16),
                   jax.ShapeDtypeStruct((nb, seq, A_W), BF16),
                   jax.ShapeDtypeStruct((nb, seq, A_W), kv_dtype),
                   jax.ShapeDtypeStruct((nb, seq, A_W), kv_dtype)),
        grid=(nb,),
        scratch_shapes=[pltpu.VMEM((seq, A_W), BF16)] if context else [],
        in_specs=[tok(D), _mod_spec(layer, row_of_batch), _layer_spec(pp["norm1_g"].shape, layer),
                  _layer_spec(pp["even_w_in"].shape, i), _layer_spec(pp["even_conv_w"].shape, i),
                  _layer_spec(pp["na_q_norm"].shape, i), _layer_spec(pp["na_k_norm"].shape, i)],
        out_specs=(tok(A_W), tok(A_W), tok(A_W), tok(A_W)),
        compiler_params=_params(1), name="even_in",
    )(y, mods, pp["norm1_g"], pp["even_w_in"], pp["even_conv_w"], pp["na_q_norm"], pp["na_k_norm"])


def _odd_in_call(y, mods, layer, i, row_of_batch, pp, rope_tabs, context):
    nb, seq, _ = y.shape
    tok = lambda w: pl.BlockSpec((None, seq, w), lambda b: (b, 0, 0))
    rope = rope_tabs is not None
    names = ["norm1_g", "odd_w_in", "pool_w", "pool_scale", "q_a_norm",
             "w_q_b_rope" if rope else "w_q_b", "kv_a_norm", "w_kv_b", "mla_q_norm", "mla_k_norm"]
    layers = [layer] + [i] * 9
    in_specs = [tok(D), _mod_spec(layer, row_of_batch)]
    in_specs += [_layer_spec(pp[n].shape, li) for n, li in zip(names, layers)]
    args = [y, mods] + [pp[n] for n in names]
    if rope:
        in_specs += [pl.BlockSpec((seq, MLA_HP), lambda b: (0, 0))] * 2
        in_specs += [_layer_spec(pp["mla_q_norm_partner"].shape, i)]
        args += list(rope_tabs) + [pp["mla_q_norm_partner"]]
    qk_w, v_w = MLA_HEADS * MLA_HP, MLA_HEADS * V_DIM
    if context:
        widths, dtypes = [A_W, v_w, KV_LORA, MLA_HP], [BF16, BF16, F32, F32]
        scratch = [pltpu.VMEM((seq, qk_w), BF16), pltpu.VMEM((seq, qk_w), BF16),
                   pltpu.VMEM((seq, v_w), BF16)]
    else:
        widths, dtypes = [A_W, qk_w, qk_w, v_w], [BF16] * 4
        scratch = []
    return pl.pallas_call(
        functools.partial(_odd_in_kernel, seq=seq, rope=rope, context=context),
        out_shape=tuple(jax.ShapeDtypeStruct((nb, seq, w), dt) for w, dt in zip(widths, dtypes)),
        grid=(nb,), in_specs=in_specs, out_specs=tuple(tok(w) for w in widths),
        scratch_shapes=scratch, compiler_params=_params(1), name="odd_in",
    )(*args)


def _cache_kv_call(cache_ckv, cache_kpe_slot, i, pp):
    nb, _, past, _ = cache_ckv.shape
    cache = lambda w: pl.BlockSpec((None, None, past, w), lambda b: (b, i, 0, 0))
    tok = lambda w: pl.BlockSpec((None, past, w), lambda b: (b, 0, 0))
    return pl.pallas_call(
        _cache_kv_kernel,
        out_shape=(jax.ShapeDtypeStruct((nb, past, MLA_HEADS * MLA_HP), BF16),
                   jax.ShapeDtypeStruct((nb, past, MLA_HEADS * V_DIM), BF16)),
        grid=(nb,),
        in_specs=[cache(KV_LORA), cache(MLA_HP), _layer_spec(pp["w_kv_b"].shape, i),
                  _layer_spec(pp["mla_k_norm"].shape, i)],
        out_specs=(tok(MLA_HEADS * MLA_HP), tok(MLA_HEADS * V_DIM)),
        compiler_params=_params(1), name="cache_kv",
    )(cache_ckv, cache_kpe_slot, pp["w_kv_b"], pp["mla_k_norm"])


def _attn_call(q, kvs, dk, pairs_per_step):
    nb, lq, _ = q.shape
    n_groups = NA_HEADS // (2 * pairs_per_step)
    qk_w, v_w = 2 * dk * pairs_per_step, LANES * pairs_per_step
    in_specs = [pl.BlockSpec((None, lq, qk_w), lambda g, b: (b, 0, g))]
    args = [q]
    for k, v in kvs:
        in_specs += [pl.BlockSpec((None, k.shape[1], qk_w), lambda g, b: (b, 0, g)),
                     pl.BlockSpec((None, v.shape[1], v_w), lambda g, b: (b, 0, g))]
        args += [k, v]
    return pl.pallas_call(
        functools.partial(_attn_kernel, lq=lq, dk=dk, n_pairs=pairs_per_step, n_src=len(kvs)),
        out_shape=jax.ShapeDtypeStruct((nb, lq, NA_HEADS * V_DIM), BF16),
        grid=(n_groups, nb), in_specs=in_specs,
        out_specs=pl.BlockSpec((None, lq, v_w), lambda g, b: (b, 0, g)),
        compiler_params=_params(2), name="attn",
    )(*args)


def _na_call(q, k, v, cache_k, cache_v, i, pair_bias):
    nb, lq, _ = q.shape
    past = cache_k.shape[2]
    tok = pl.BlockSpec((None, lq, LANES), lambda g, b: (b, 0, g))
    cache = pl.BlockSpec((None, None, past, LANES), lambda g, b: (b, i, 0, g))
    return pl.pallas_call(
        _na_kernel,
        out_shape=jax.ShapeDtypeStruct((nb, lq, NA_HEADS * NA_DH), BF16),
        grid=(NA_HEADS // 2, nb),
        in_specs=[tok, tok, tok, cache, cache,
                  pl.BlockSpec((None, 2, 2 * KH, GRID_W, LANES), lambda g, b: (i, g, 0, 0, 0))],
        out_specs=tok,
        scratch_shapes=[pltpu.VMEM((2, lq, NA_WIN_ROWS * GRID_W), F32)],
        compiler_params=_params(2), name="na_attn",
    )(q, k, v, cache_k, cache_v, pair_bias)


def _out_mlp_call(y, b1, b2, mods, layer, row_of_batch, w_out, w_out_layer, pp):
    nb, seq, _ = y.shape
    tm = 1024
    tok = lambda w: pl.BlockSpec((None, tm, w), lambda b, t: (b, t, 0))
    return pl.pallas_call(
        functools.partial(_out_mlp_kernel, hc=512),
        out_shape=jax.ShapeDtypeStruct(y.shape, F32),
        grid=(nb, seq // tm),
        in_specs=[tok(D), tok(A_W), tok(A_W), _mod_spec(layer, row_of_batch),
                  _layer_spec(w_out.shape, w_out_layer), _layer_spec(pp["norm2_g"].shape, layer),
                  _layer_spec(pp["mlp_w1"].shape, layer), _layer_spec(pp["mlp_w2"].shape, layer)],
        out_specs=tok(D),
        compiler_params=_params(2), name="out_mlp",
    )(y, b1, b2, mods, w_out, pp["norm2_g"], pp["mlp_w1"], pp["mlp_w2"])


def _row(v):
    return v[:, None, :].astype(F32)


def _head_slot(v):
    return _row(jnp.pad(v, ((0, 0), (0, MLA_HP - QK_DIM))))


def _swap_rotary_halves(v):
    half = ROPE // 2
    return jnp.concatenate(
        [jnp.zeros_like(v[..., :NOPE]), v[..., NOPE + half:], v[..., NOPE:NOPE + half]], axis=-1)


def _prepare(p):
    n_odd = p["odd_w_in"].shape[0]
    pp = {}
    pp["norm1_g"], pp["norm2_g"] = _row(p["norm1_g"]), _row(p["norm2_g"])
    pp["mlp_w1"], pp["mlp_w2"] = p["mlp_w1"].astype(BF16), p["mlp_w2"].astype(BF16)
    pp["even_w_in"] = p["even_w_in"].astype(BF16)
    pp["even_conv_w"] = p["even_conv_w"].astype(F32)
    pp["na_q_norm"] = _row(jnp.tile(p["na_q_norm"], (1, NA_HEADS)))
    pp["na_k_norm"] = _row(jnp.tile(p["na_k_norm"], (1, NA_HEADS)))
    pp["even_w_out"] = p["even_w_out"].astype(BF16)
    pp["odd_w_out"] = p["odd_w_out"].astype(BF16)
    w = p["odd_w_in"]
    o_pe = A_W + Q_LORA + KV_LORA
    pe = jnp.pad(w[:, :, o_pe:], ((0, 0), (0, 0), (NOPE, MLA_HP - QK_DIM)))
    pp["odd_w_in"] = jnp.concatenate([w[:, :, :o_pe], pe], axis=-1).astype(BF16)
    pp["pool_w"] = p["pool_w"].astype(BF16)
    pp["pool_scale"] = _row(p["pool_scale"])
    pp["q_a_norm"], pp["kv_a_norm"] = _row(p["q_a_norm"]), _row(p["kv_a_norm"])
    slot_pad = ((0, 0), (0, 0), (0, 0), (0, MLA_HP - QK_DIM))
    wq = p["w_q_b"].reshape(n_odd, Q_LORA, MLA_HEADS, QK_DIM)
    wq_slots = jnp.pad(wq, slot_pad).reshape(n_odd, Q_LORA, MLA_HEADS * MLA_HP)
    wq_partner = jnp.pad(_swap_rotary_halves(wq), slot_pad).reshape(n_odd, Q_LORA, MLA_HEADS * MLA_HP)
    pp["w_q_b"] = wq_slots.astype(BF16)
    pp["w_q_b_rope"] = jnp.concatenate([wq_slots, wq_partner], axis=-1).astype(BF16)
    wkv = p["w_kv_b"].reshape(n_odd, KV_LORA, MLA_HEADS, NOPE + V_DIM)
    wk = jnp.pad(wkv[..., :NOPE], ((0, 0), (0, 0), (0, 0), (0, MLA_HP - NOPE)))
    pp["w_kv_b"] = jnp.concatenate(
        [wk.reshape(n_odd, KV_LORA, MLA_HEADS * MLA_HP),
         wkv[..., NOPE:].reshape(n_odd, KV_LORA, MLA_HEADS * V_DIM)], axis=-1).astype(BF16)
    pp["mla_q_norm"], pp["mla_k_norm"] = _head_slot(p["mla_q_norm"]), _head_slot(p["mla_k_norm"])
    pp["mla_q_norm_partner"] = _head_slot(_swap_rotary_halves(p["mla_q_norm"]))
    return pp


def _rope_tables(n):
    t = jnp.arange(n)
    row = (t // GRID_W).astype(F32)
    col = (t % GRID_W).astype(F32)
    axis_dim = ROPE // 2
    inv = 1.0 / (ROPE_BASE ** (jnp.arange(0, axis_dim, 2, dtype=F32) / axis_dim))
    ang = jnp.concatenate([row[:, None] * inv, col[:, None] * inv], axis=-1)
    cos, sin = jnp.cos(ang), jnp.sin(ang)
    tail = jnp.zeros((n, MLA_HP - QK_DIM), F32)
    rc = jnp.concatenate([jnp.ones((n, NOPE), F32), cos, cos, tail], axis=-1)
    rs = jnp.concatenate([jnp.zeros((n, NOPE), F32), -sin, sin, tail], axis=-1)
    return rc, rs


def _na_pair_bias(rpb):
    c = np.arange(GRID_W)
    start = np.clip(c - KW // 2, 0, GRID_W - KW)
    kc = np.arange(GRID_W)
    valid = (kc[None, :] >= start[:, None]) & (kc[None, :] < start[:, None] + KW)
    dc = kc[None, :] - c[:, None] + KW - 1
    onehot = ((dc[None] == np.arange(2 * KW - 1)[:, None, None]) & valid[None]).astype(np.float32)
    blocks = jnp.einsum("lhij,jck->lhick", rpb.astype(F32), onehot, precision=lax.Precision.HIGHEST)
    blocks = jnp.where(valid, blocks, NEG_INF)
    blocks = jnp.pad(blocks, ((0, 0), (0, 0), (1, 1), (0, 0), (0, 0)), constant_values=NEG_INF)
    return jnp.concatenate([blocks[:, :, :-1], blocks[:, :, 1:]], axis=-1).astype(F32)


def kernel(x_prompt, x_sample, cache_na_k, cache_na_v, cache_mla_ckv, cache_mla_kpe, c, c_ctx,
           ada_w, ada_b, norm1_g, norm2_g, mlp_w1, mlp_w2,
           even_w_in, even_conv_w, na_q_norm, na_k_norm, na_rpb, even_w_out,
           odd_w_in, pool_w, pool_scale, q_a_norm, w_q_b, kv_a_norm, w_kv_b,
           mla_q_norm, mla_k_norm, odd_w_out):
    bp, lp, _ = x_prompt.shape
    bs, ls, _ = x_sample.shape
    pp = _prepare(dict(norm1_g=norm1_g, norm2_g=norm2_g, mlp_w1=mlp_w1, mlp_w2=mlp_w2,
                       even_w_in=even_w_in, even_conv_w=even_conv_w, na_q_norm=na_q_norm,
                       na_k_norm=na_k_norm, even_w_out=even_w_out, odd_w_in=odd_w_in, pool_w=pool_w,
                       pool_scale=pool_scale, q_a_norm=q_a_norm, w_q_b=w_q_b, kv_a_norm=kv_a_norm,
                       w_kv_b=w_kv_b, mla_q_norm=mla_q_norm, mla_k_norm=mla_k_norm,
                       odd_w_out=odd_w_out))
    c_all = jnp.concatenate(
        [c_ctx[None, :], c, jnp.zeros((MOD_ROWS - 1 - bs, D), F32)], axis=0)
    mods = _ada_call(c_all, ada_w, ada_b[:, None, :]).reshape(DEPTH, MOD_ROWS, 1, 6 * D)
    rope_tabs = _rope_tables(ls)
    pair_bias = _na_pair_bias(na_rpb)
    past = cache_na_k.shape[2]
    cache_k = cache_na_k.reshape(bs, -1, past, NA_HEADS * NA_DH)
    cache_v = cache_na_v.reshape(bs, -1, past, NA_HEADS * NA_DH)
    cache_kpe_slot = jnp.pad(cache_mla_kpe, ((0, 0), (0, 0), (0, 0), (NOPE, MLA_HP - QK_DIM)))
    ctx_row = lambda b: 0
    lat_row = lambda b: b + 1

    yp, ys = x_prompt, x_sample
    na_k, na_v, mla_ckv, mla_kpe = [], [], [], []
    for l in range(DEPTH):
        i = l // 2
        if l % 2 == 0:
            ap, att_p, kp, vp = _even_in_call(yp, mods, l, i, ctx_row, pp, True)
            na_k.append(kp.reshape(bp, lp, NA_HEADS, NA_DH))
            na_v.append(vp.reshape(bp, lp, NA_HEADS, NA_DH))
            a_s, qs, k_s, vs = _even_in_call(ys, mods, l, i, lat_row, pp, False)
            att_s = _na_call(qs, k_s, vs, cache_k, cache_v, i, pair_bias)
            w_out = pp["even_w_out"]
        else:
            ap, att_p, ckvp, kpep = _odd_in_call(yp, mods, l, i, ctx_row, pp, None, True)
            mla_ckv.append(ckvp)
            mla_kpe.append(kpep[:, :, NOPE:QK_DIM])
            a_s, qs, k_s, vs = _odd_in_call(ys, mods, l, i, lat_row, pp, rope_tabs, False)
            kc, vc = _cache_kv_call(cache_mla_ckv, cache_kpe_slot, i, pp)
            att_s = _attn_call(qs, [(kc, vc), (k_s, vs)], MLA_HP, 1)
            w_out = pp["odd_w_out"]
        yp = _out_mlp_call(yp.reshape(1, bp * lp, D), ap.reshape(1, bp * lp, A_W),
                           att_p.reshape(1, bp * lp, A_W), mods, l, ctx_row, w_out, i,
                           pp).reshape(bp, lp, D)
        ys = _out_mlp_call(ys, a_s, att_s, mods, l, lat_row, w_out, i, pp)
    return (yp, ys, jnp.stack(na_k, axis=1), jnp.stack(na_v, axis=1),
            jnp.stack(mla_ckv, axis=1), jnp.stack(mla_kpe, axis=1))
```

```python
import functools

import jax
import jax.numpy as jnp
import numpy as np
from jax import lax
from jax.experimental import pallas as pl
from jax.experimental.pallas import tpu as pltpu

F32 = jnp.float32
BF16 = jnp.bfloat16

D = 1024
DEPTH = 4
GRID_W = 64
GRID_R = 16
A_W = 512
NA_HEADS = 8
NA_DH = 64
KH = 8
KW = 16
POOL_WINDOWS = (2, 4, 8, 16)
POOL_G = 128
POOL_PAD = 16
MLA_HEADS = 8
NOPE = 64
ROPE = 32
QK_DIM = NOPE + ROPE
MLA_HP = 128
V_DIM = 64
Q_LORA = 384
KV_LORA = 256
ROPE_BASE = 10000.0
MLP_HIDDEN = 4 * D
NEG_INF = -1e30
EPS = 1e-6
MOD_ROWS = 16
LANES = 128
NA_WIN_ROWS = 12
Q_CHUNK = 256
VMEM_LIMIT = 56 * 1024 * 1024


def _dot(a, b):
    return jnp.dot(a, b, preferred_element_type=F32)


def _dot_nt(a, b):
    return lax.dot_general(a, b, (((1,), (1,)), ((), ())), preferred_element_type=F32)


def _rms(x, g):
    ms = jnp.mean(x * x, axis=-1, keepdims=True)
    return x * lax.rsqrt(ms + EPS) * g


def _modulated(x, g, shift, scale):
    return _rms(x, g) * (1.0 + scale) + shift


def _head_rms_64(x, g):
    t_rows = x.shape[0]
    low = lax.broadcasted_iota(jnp.int32, (t_rows, LANES), 1) < NA_DH
    cols = []
    for j in range(x.shape[1] // LANES):
        t = x[:, j * LANES:(j + 1) * LANES]
        t2 = t * t
        s_lo = jnp.sum(jnp.where(low, t2, 0.0), axis=-1, keepdims=True)
        s_hi = jnp.sum(jnp.where(low, 0.0, t2), axis=-1, keepdims=True)
        r_lo = lax.rsqrt(s_lo * (1.0 / NA_DH) + EPS)
        r_hi = lax.rsqrt(s_hi * (1.0 / NA_DH) + EPS)
        cols.append(t * jnp.where(low, r_lo, r_hi))
    return jnp.concatenate(cols, axis=-1) * g


def _slot_inv_rms(t):
    return lax.rsqrt(jnp.sum(t * t, axis=-1, keepdims=True) * (1.0 / QK_DIM) + EPS)


def _rotary_partner(t):
    x1_side = lax.broadcasted_iota(jnp.int32, t.shape, 1) < NOPE + ROPE // 2
    return jnp.where(x1_side, pltpu.roll(t, LANES - ROPE // 2, 1), pltpu.roll(t, ROPE // 2, 1))


def _mla_q(qf, q_partner, g_slot, g_partner, rope_tabs):
    if rope_tabs is not None:
        rc, rs = rope_tabs
        gc, gs = g_slot * rc, g_partner * rs
    cols = []
    for j in range(MLA_HEADS):
        t = qf[:, j * MLA_HP:(j + 1) * MLA_HP]
        if rope_tabs is None:
            u = t * g_slot
        else:
            u = t * gc + q_partner[:, j * MLA_HP:(j + 1) * MLA_HP] * gs
        cols.append(u * (_slot_inv_rms(t) * (QK_DIM ** -0.5)))
    return jnp.concatenate(cols, axis=-1)


def _kv_up(ckv, kpe_slot, wkvb_ref, g_slot, rope_tabs):
    kvb = _dot(ckv.astype(BF16), wkvb_ref[...])
    kr = kpe_slot * g_slot
    if rope_tabs is not None:
        rc, rs = rope_tabs
        kr = kr * rc + _rotary_partner(kr) * rs
    cols = []
    for j in range(MLA_HEADS):
        kn = kvb[:, j * MLA_HP:(j + 1) * MLA_HP]
        cols.append((kn * g_slot + kr) * _slot_inv_rms(kn + kpe_slot))
    return jnp.concatenate(cols, axis=-1), kvb[:, MLA_HEADS * MLA_HP:]


def _ada_kernel(c_ref, w_ref, b_ref, o_ref):
    c = c_ref[...]
    s = c * (1.0 / (1.0 + jnp.exp(-c)))
    o_ref[...] = _dot(s.astype(BF16), w_ref[...].astype(BF16)) + b_ref[...]


def _even_in_kernel(y_ref, mod_ref, g1_ref, w_ref, cw_ref, gq_ref, gk_ref, *out_refs, seq, context):
    if context:
        a_ref, att_ref, kt_ref, vt_ref, q_ref, k_ref, v_ref = out_refs
    else:
        a_ref, q_ref, k_ref, v_ref = out_refs
    mod = mod_ref[...]
    h = _modulated(y_ref[...], g1_ref[...], mod[:, 0:D], mod[:, D:2 * D])
    z = _dot(h.astype(BF16), w_ref[...])
    u = z[:, A_W:2 * A_W] * z[:, 2 * A_W:3 * A_W]
    t = lax.broadcasted_iota(jnp.int32, (seq, A_W), 0)
    prev = jnp.where(t > 0, pltpu.roll(u, 1, 0), 0.0)
    nxt = jnp.where(t < seq - 1, pltpu.roll(u, seq - 1, 0), 0.0)
    cw = cw_ref[...]
    a = z[:, 0:A_W] * (prev * cw[0:1] + u * cw[1:2] + nxt * cw[2:3])
    a_ref[...] = a.astype(BF16)
    q = _head_rms_64(z[:, 3 * A_W:4 * A_W], gq_ref[...]) * (NA_DH ** -0.5)
    q_ref[...] = q.astype(BF16)
    k = _head_rms_64(z[:, 4 * A_W:5 * A_W], gk_ref[...])
    v = z[:, 5 * A_W:6 * A_W]
    k_ref[...] = k.astype(BF16)
    v_ref[...] = v.astype(BF16)
    if context:
        kt_ref[...] = k.T
        vt_ref[...] = v.T
        _attend_all(q_ref, [k_ref, v_ref], att_ref, seq, NA_DH, NA_HEADS // 2)


def _pool_branch(u, pw_ref, ps, seq):
    zpad = jnp.zeros((POOL_PAD, POOL_G), F32)
    padded = seq + 2 * POOL_PAD
    t = lax.broadcasted_iota(jnp.int32, (seq, POOL_G), 0)
    cols = []
    for g, w in enumerate(POOL_WINDOWS):
        ug = u[:, g * POOL_G:(g + 1) * POOL_G]
        xp = jnp.concatenate([zpad, ug, zpad], axis=0)
        acc = xp + pltpu.roll(xp, 1, 0)
        s = 1
        while 4 * s <= w:
            acc = pltpu.roll(acc, s, 0) + pltpu.roll(acc, padded - s, 0)
            s *= 2
        win = acc[POOL_PAD:POOL_PAD + seq]
        lo = jnp.maximum(t - w // 2, 0)
        hi = jnp.minimum(t - w // 2 + w, seq)
        pooled = win / (hi - lo).astype(F32)
        cols.append(_dot((pooled - ug).astype(BF16), pw_ref[g]))
    return jnp.concatenate(cols, axis=-1) * ps


def _odd_in_kernel(*refs, seq, rope, context):
    (y_ref, mod_ref, g1_ref, w_ref, pw_ref, ps_ref, qan_ref, wqb_ref, kvan_ref, wkvb_ref,
     gq_ref, gk_ref) = refs[:12]
    pos = 12
    rope_tabs = g_partner = q_partner = None
    if rope:
        rope_tabs = (refs[pos][...], refs[pos + 1][...])
        g_partner = refs[pos + 2][...]
        pos += 3
    if context:
        p_ref, att_ref, ckv_ref, kpe_ref, q_ref, k_ref, v_ref = refs[pos:pos + 7]
    else:
        p_ref, q_ref, k_ref, v_ref = refs[pos:pos + 4]
    mod = mod_ref[...]
    h = _modulated(y_ref[...], g1_ref[...], mod[:, 0:D], mod[:, D:2 * D])
    z = _dot(h.astype(BF16), w_ref[...])
    p_ref[...] = _pool_branch(z[:, 0:A_W], pw_ref, ps_ref[...], seq).astype(BF16)
    o_q, o_kv, o_pe = A_W, A_W + Q_LORA, A_W + Q_LORA + KV_LORA
    ql = _rms(z[:, o_q:o_kv], qan_ref[...])
    qf = _dot(ql.astype(BF16), wqb_ref[...])
    if rope:
        q_partner = qf[:, MLA_HEADS * MLA_HP:]
    q_ref[...] = _mla_q(qf, q_partner, gq_ref[...], g_partner, rope_tabs).astype(BF16)
    ckv = _rms(z[:, o_kv:o_pe], kvan_ref[...])
    kpe_slot = z[:, o_pe:o_pe + MLA_HP]
    k, v = _kv_up(ckv, kpe_slot, wkvb_ref, gk_ref[...], rope_tabs)
    k_ref[...] = k.astype(BF16)
    v_ref[...] = v.astype(BF16)
    if context:
        ckv_ref[...] = ckv
        kpe_ref[...] = kpe_slot
        _attend_all(q_ref, [k_ref, v_ref], att_ref, seq, MLA_HP, MLA_HEADS // 2)


def _cache_kv_kernel(ckv_ref, kpe_ref, wkvb_ref, gk_ref, k_ref, v_ref):
    k, v = _kv_up(ckv_ref[...], kpe_ref[...], wkvb_ref, gk_ref[...], None)
    k_ref[...] = k.astype(BF16)
    v_ref[...] = v.astype(BF16)


def _attend(q, sources):
    scores = []
    for k, _, bias in sources:
        s = _dot_nt(q, k)
        scores.append(s if bias is None else s + bias)
    m = scores[0].max(axis=-1, keepdims=True)
    for s in scores[1:]:
        m = jnp.maximum(m, s.max(axis=-1, keepdims=True))
    den = None
    out = None
    for s, (_, v, _) in zip(scores, sources):
        p = jnp.exp(s - m)
        ps = p.sum(axis=-1, keepdims=True)
        pv = _dot(p.astype(BF16), v)
        den = ps if den is None else den + ps
        out = pv if out is None else out + pv
    return out / den


def _pair_outputs(q_ref, q_rows, pair, dk, srcs, n_rows):
    low = lax.broadcasted_iota(jnp.int32, (n_rows, LANES), 1) < V_DIM
    vcols = slice(pair * LANES, (pair + 1) * LANES)
    outs = []
    for hh in range(2):
        if dk == NA_DH:
            q128 = q_ref[q_rows, pair * LANES:(pair + 1) * LANES]
            keep = low if hh == 0 else jnp.logical_not(low)
            q = jnp.where(keep, q128, jnp.zeros_like(q128))
            kcols = slice(pair * LANES, (pair + 1) * LANES)
        else:
            kcols = slice((2 * pair + hh) * dk, (2 * pair + hh + 1) * dk)
            q = q_ref[q_rows, kcols]
        sources = []
        for k_ref, k_rows, v_ref, v_rows, bias_pair in srcs:
            k = k_ref[k_rows, kcols].astype(BF16)
            v = v_ref[v_rows, vcols].astype(BF16)
            sources.append((k, v, None if bias_pair is None else bias_pair[hh]))
        outs.append(_attend(q, sources))
    return jnp.where(low, outs[0], outs[1])


def _attend_all(q_ref, kv, o_ref, lq, dk, n_pairs):
    full = slice(None)
    for c in range(lq // Q_CHUNK):
        rows = slice(c * Q_CHUNK, (c + 1) * Q_CHUNK)
        for pair in range(n_pairs):
            srcs = [(kv[2 * i], full, kv[2 * i + 1], full, None) for i in range(len(kv) // 2)]
            o = _pair_outputs(q_ref, rows, pair, dk, srcs, Q_CHUNK)
            o_ref[rows, pair * LANES:(pair + 1) * LANES] = o.astype(o_ref.dtype)


def _attn_kernel(*refs, lq, dk, n_pairs, n_src):
    _attend_all(refs[0], refs[1:1 + 2 * n_src], refs[1 + 2 * n_src], lq, dk, n_pairs)


def _na_kernel(q_ref, k_ref, v_ref, ck_ref, cv_ref, pb_ref, o_ref, bias_ref, *, n_pairs):
    @pl.when(pl.program_id(1) == 0)
    def _():
        low = lax.broadcasted_iota(jnp.int32, (GRID_W, LANES), 1) < GRID_W
        neg = jnp.full((GRID_W, LANES), NEG_INF, F32)
        for hh in range(2 * n_pairs):
            for r in range(GRID_R):
                win0 = (r // 8) * 4
                rs = min(max(r - KH // 2, 0), GRID_R - KH)
                delta, dr0 = rs - win0, rs - r + KH - 1
                for p in range(NA_WIN_ROWS // 2):
                    ok0 = delta <= 2 * p < delta + KH
                    ok1 = delta <= 2 * p + 1 < delta + KH
                    if not (ok0 or ok1):
                        blk = neg
                    else:
                        blk = pb_ref[hh, dr0 + 2 * p - delta + 1]
                        if not ok0:
                            blk = jnp.where(low, NEG_INF, blk)
                        if not ok1:
                            blk = jnp.where(low, blk, NEG_INF)
                    bias_ref[hh, r * GRID_W:(r + 1) * GRID_W, p * LANES:(p + 1) * LANES] = blk

    full = slice(None)
    for c in range(GRID_R * GRID_W // Q_CHUNK):
        rows = slice(c * Q_CHUNK, (c + 1) * Q_CHUNK)
        row_lo = min(max(4 * c - KH // 2, 0), GRID_R - KH)
        row_hi = min(max(4 * c + 3 - KH // 2, 0), GRID_R - KH) + KH
        row_hi += (row_hi - row_lo) % 2
        win = slice(row_lo * GRID_W, row_hi * GRID_W)
        strip = slice((row_lo - (c // 2) * 4) * GRID_W, (row_hi - (c // 2) * 4) * GRID_W)
        for pair in range(n_pairs):
            bias_pair = (bias_ref[2 * pair, rows, strip], bias_ref[2 * pair + 1, rows, strip])
            srcs = [(k_ref, win, v_ref, win, bias_pair), (ck_ref, full, cv_ref, full, None)]
            o = _pair_outputs(q_ref, rows, pair, NA_DH, srcs, Q_CHUNK)
            o_ref[rows, pair * LANES:(pair + 1) * LANES] = o.astype(o_ref.dtype)


def _out_mlp_kernel(y_ref, b1_ref, b2_ref, mod_ref, wo_ref, g2_ref, w1_ref, w2_ref, o_ref, *, hc):
    mod = mod_ref[...]
    mix = _dot(b1_ref[...], wo_ref[0:A_W, :]) + _dot(b2_ref[...], wo_ref[A_W:2 * A_W, :])
    y1 = y_ref[...] + mod[:, 2 * D:3 * D] * mix
    h2 = _modulated(y1, g2_ref[...], mod[:, 3 * D:4 * D], mod[:, 4 * D:5 * D]).astype(BF16)
    acc = jnp.zeros(y1.shape, F32)
    for c in range(MLP_HIDDEN // hc):
        a = jnp.maximum(_dot(h2, w1_ref[:, c * hc:(c + 1) * hc]), 0.0)
        acc = acc + _dot((a * a).astype(BF16), w2_ref[c * hc:(c + 1) * hc, :])
    o_ref[...] = y1 + mod[:, 5 * D:6 * D] * acc


def _params(n_grid):
    return pltpu.CompilerParams(dimension_semantics=("arbitrary",) * n_grid,
                                vmem_limit_bytes=VMEM_LIMIT)


def _layer_spec(shape, layer):
    nd = len(shape)
    return pl.BlockSpec((None,) + tuple(shape[1:]), lambda *_: (layer,) + (0,) * (nd - 1),
                        pipeline_mode=pl.Buffered(1))


def _mod_spec(layer, row_of_batch):
    return pl.BlockSpec((None, None, 1, 6 * D), lambda b, *_: (layer, row_of_batch(b), 0, 0))


def _ada_call(c_all, ada_w, ada_b):
    tn = 1536
    return pl.pallas_call(
        _ada_kernel,
        out_shape=jax.ShapeDtypeStruct((DEPTH, MOD_ROWS, 6 * D), F32),
        grid=(DEPTH, 6 * D // tn),
        in_specs=[pl.BlockSpec((MOD_ROWS, D), lambda l, j: (0, 0)),
                  pl.BlockSpec((None, D, tn), lambda l, j: (l, 0, j)),
                  pl.BlockSpec((None, 1, tn), lambda l, j: (l, 0, j))],
        out_specs=pl.BlockSpec((None, MOD_ROWS, tn), lambda l, j: (l, 0, j)),
        compiler_params=_params(2), name="ada_mod",
    )(c_all, ada_w, ada_b)


def _even_in_call(y, mods, layer, i, row_of_batch, pp, context):
    nb, seq, _ = y.shape
    tok = lambda w: pl.BlockSpec((None, seq, w), lambda b: (b, 0, 0))
    tok_t = pl.BlockSpec((None, A_W, seq), lambda b: (b, 0, 0))
    state = jax.ShapeDtypeStruct((nb, A_W, seq), F32)
    act = jax.ShapeDtypeStruct((nb, seq, A_W), BF16)
    return pl.pallas_call(
        functools.partial(_even_in_kernel, seq=seq, context=context),
        out_shape=(act, act, state, state) if context else (act,) * 4,
        grid=(nb,),
        scratch_shapes=[pltpu.VMEM((seq, A_W), BF16)] * 3 if context else [],
        in_specs=[tok(D), _mod_spec(layer, row_of_batch), _layer_spec(pp["norm1_g"].shape, layer),
                  _layer_spec(pp["even_w_in"].shape, i), _layer_spec(pp["even_conv_w"].shape, i),
                  _layer_spec(pp["na_q_norm"].shape, i), _layer_spec(pp["na_k_norm"].shape, i)],
        out_specs=(tok(A_W), tok(A_W), tok_t, tok_t) if context else (tok(A_W),) * 4,
        compiler_params=_params(1), name="even_in",
    )(y, mods, pp["norm1_g"], pp["even_w_in"], pp["even_conv_w"], pp["na_q_norm"], pp["na_k_norm"])


def _odd_in_call(y, mods, layer, i, row_of_batch, pp, rope_tabs, context):
    nb, seq, _ = y.shape
    tok = lambda w: pl.BlockSpec((None, seq, w), lambda b: (b, 0, 0))
    rope = rope_tabs is not None
    names = ["norm1_g", "odd_w_in", "pool_w", "pool_scale", "q_a_norm",
             "w_q_b_rope" if rope else "w_q_b", "kv_a_norm", "w_kv_b", "mla_q_norm", "mla_k_norm"]
    layers = [layer] + [i] * 9
    in_specs = [tok(D), _mod_spec(layer, row_of_batch)]
    in_specs += [_layer_spec(pp[n].shape, li) for n, li in zip(names, layers)]
    args = [y, mods] + [pp[n] for n in names]
    if rope:
        in_specs += [pl.BlockSpec((seq, MLA_HP), lambda b: (0, 0))] * 2
        in_specs += [_layer_spec(pp["mla_q_norm_partner"].shape, i)]
        args += list(rope_tabs) + [pp["mla_q_norm_partner"]]
    qk_w, v_w = MLA_HEADS * MLA_HP, MLA_HEADS * V_DIM
    if context:
        widths, dtypes = [A_W, v_w, KV_LORA, MLA_HP], [BF16, BF16, F32, F32]
        scratch = [pltpu.VMEM((seq, qk_w), BF16), pltpu.VMEM((seq, qk_w), BF16),
                   pltpu.VMEM((seq, v_w), BF16)]
    else:
        widths, dtypes = [A_W, qk_w, qk_w, v_w], [BF16] * 4
        scratch = []
    return pl.pallas_call(
        functools.partial(_odd_in_kernel, seq=seq, rope=rope, context=context),
        out_shape=tuple(jax.ShapeDtypeStruct((nb, seq, w), dt) for w, dt in zip(widths, dtypes)),
        grid=(nb,), in_specs=in_specs, out_specs=tuple(tok(w) for w in widths),
        scratch_shapes=scratch, compiler_params=_params(1), name="odd_in",
    )(*args)


def _cache_kv_call(cache_ckv, cache_kpe_slot, i, pp):
    nb, _, past, _ = cache_ckv.shape
    cache = lambda w: pl.BlockSpec((None, None, past, w), lambda b: (b, i, 0, 0))
    tok = lambda w: pl.BlockSpec((None, past, w), lambda b: (b, 0, 0))
    return pl.pallas_call(
        _cache_kv_kernel,
        out_shape=(jax.ShapeDtypeStruct((nb, past, MLA_HEADS * MLA_HP), BF16),
                   jax.ShapeDtypeStruct((nb, past, MLA_HEADS * V_DIM), BF16)),
        grid=(nb,),
        in_specs=[cache(KV_LORA), cache(MLA_HP), _layer_spec(pp["w_kv_b"].shape, i),
                  _layer_spec(pp["mla_k_norm"].shape, i)],
        out_specs=(tok(MLA_HEADS * MLA_HP), tok(MLA_HEADS * V_DIM)),
        compiler_params=_params(1), name="cache_kv",
    )(cache_ckv, cache_kpe_slot, pp["w_kv_b"], pp["mla_k_norm"])


def _attn_call(q, kvs, dk, pairs_per_step):
    nb, lq, _ = q.shape
    n_groups = NA_HEADS // (2 * pairs_per_step)
    qk_w, v_w = 2 * dk * pairs_per_step, LANES * pairs_per_step
    in_specs = [pl.BlockSpec((None, lq, qk_w), lambda g, b: (b, 0, g))]
    args = [q]
    for k, v in kvs:
        in_specs += [pl.BlockSpec((None, k.shape[1], qk_w), lambda g, b: (b, 0, g)),
                     pl.BlockSpec((None, v.shape[1], v_w), lambda g, b: (b, 0, g))]
        args += [k, v]
    return pl.pallas_call(
        functools.partial(_attn_kernel, lq=lq, dk=dk, n_pairs=pairs_per_step, n_src=len(kvs)),
        out_shape=jax.ShapeDtypeStruct((nb, lq, NA_HEADS * V_DIM), BF16),
        grid=(n_groups, nb), in_specs=in_specs,
        out_specs=pl.BlockSpec((None, lq, v_w), lambda g, b: (b, 0, g)),
        compiler_params=_params(2), name="attn",
    )(*args)


def _na_call(q, k, v, cache_k, cache_v, i, pair_bias):
    nb, lq, _ = q.shape
    past = cache_k.shape[2]
    n_pairs = 2
    width = n_pairs * LANES
    tok = pl.BlockSpec((None, lq, width), lambda g, b: (b, 0, g))
    cache = pl.BlockSpec((None, None, past, width), lambda g, b: (b, i, 0, g))
    return pl.pallas_call(
        functools.partial(_na_kernel, n_pairs=n_pairs),
        out_shape=jax.ShapeDtypeStruct((nb, lq, NA_HEADS * NA_DH), BF16),
        grid=(NA_HEADS // (2 * n_pairs), nb),
        in_specs=[tok, tok, tok, cache, cache,
                  pl.BlockSpec((None, 2 * n_pairs, 2 * KH, GRID_W, LANES),
                               lambda g, b: (i, g, 0, 0, 0))],
        out_specs=tok,
        scratch_shapes=[pltpu.VMEM((2 * n_pairs, lq, NA_WIN_ROWS * GRID_W), F32)],
        compiler_params=_params(2), name="na_attn",
    )(q, k, v, cache_k, cache_v, pair_bias)


def _out_mlp_call(y, b1, b2, mods, layer, row_of_batch, w_out, w_out_layer, pp):
    nb, seq, _ = y.shape
    tm = 1024
    tok = lambda w: pl.BlockSpec((None, tm, w), lambda b, t: (b, t, 0))
    return pl.pallas_call(
        functools.partial(_out_mlp_kernel, hc=512),
        out_shape=jax.ShapeDtypeStruct(y.shape, F32),
        grid=(nb, seq // tm),
        in_specs=[tok(D), tok(A_W), tok(A_W), _mod_spec(layer, row_of_batch),
                  _layer_spec(w_out.shape, w_out_layer), _layer_spec(pp["norm2_g"].shape, layer),
                  _layer_spec(pp["mlp_w1"].shape, layer), _layer_spec(pp["mlp_w2"].shape, layer)],
        out_specs=tok(D),
        compiler_params=_params(2), name="out_mlp",
    )(y, b1, b2, mods, w_out, pp["norm2_g"], pp["mlp_w1"], pp["mlp_w2"])


def _row(v):
    return v[:, None, :].astype(F32)


def _head_slot(v):
    return _row(jnp.pad(v, ((0, 0), (0, MLA_HP - QK_DIM))))


def _swap_rotary_halves(v):
    half = ROPE // 2
    return jnp.concatenate(
        [jnp.zeros_like(v[..., :NOPE]), v[..., NOPE + half:], v[..., NOPE:NOPE + half]], axis=-1)


def _prepare(p):
    n_odd = p["odd_w_in"].shape[0]
    pp = {}
    pp["norm1_g"], pp["norm2_g"] = _row(p["norm1_g"]), _row(p["norm2_g"])
    pp["mlp_w1"], pp["mlp_w2"] = p["mlp_w1"].astype(BF16), p["mlp_w2"].astype(BF16)
    pp["even_w_in"] = p["even_w_in"].astype(BF16)
    pp["even_conv_w"] = p["even_conv_w"].astype(F32)
    pp["na_q_norm"] = _row(jnp.tile(p["na_q_norm"], (1, NA_HEADS)))
    pp["na_k_norm"] = _row(jnp.tile(p["na_k_norm"], (1, NA_HEADS)))
    pp["even_w_out"] = p["even_w_out"].astype(BF16)
    pp["odd_w_out"] = p["odd_w_out"].astype(BF16)
    w = p["odd_w_in"]
    o_pe = A_W + Q_LORA + KV_LORA
    pe = jnp.pad(w[:, :, o_pe:], ((0, 0), (0, 0), (NOPE, MLA_HP - QK_DIM)))
    pp["odd_w_in"] = jnp.concatenate([w[:, :, :o_pe], pe], axis=-1).astype(BF16)
    pp["pool_w"] = p["pool_w"].astype(BF16)
    pp["pool_scale"] = _row(p["pool_scale"])
    pp["q_a_norm"], pp["kv_a_norm"] = _row(p["q_a_norm"]), _row(p["kv_a_norm"])
    slot_pad = ((0, 0), (0, 0), (0, 0), (0, MLA_HP - QK_DIM))
    wq = p["w_q_b"].reshape(n_odd, Q_LORA, MLA_HEADS, QK_DIM)
    wq_slots = jnp.pad(wq, slot_pad).reshape(n_odd, Q_LORA, MLA_HEADS * MLA_HP)
    wq_partner = jnp.pad(_swap_rotary_halves(wq), slot_pad).reshape(n_odd, Q_LORA, MLA_HEADS * MLA_HP)
    pp["w_q_b"] = wq_slots.astype(BF16)
    pp["w_q_b_rope"] = jnp.concatenate([wq_slots, wq_partner], axis=-1).astype(BF16)
    wkv = p["w_kv_b"].reshape(n_odd, KV_LORA, MLA_HEADS, NOPE + V_DIM)
    wk = jnp.pad(wkv[..., :NOPE], ((0, 0), (0, 0), (0, 0), (0, MLA_HP - NOPE)))
    pp["w_kv_b"] = jnp.concatenate(
        [wk.reshape(n_odd, KV_LORA, MLA_HEADS * MLA_HP),
         wkv[..., NOPE:].reshape(n_odd, KV_LORA, MLA_HEADS * V_DIM)], axis=-1).astype(BF16)
    pp["mla_q_norm"], pp["mla_k_norm"] = _head_slot(p["mla_q_norm"]), _head_slot(p["mla_k_norm"])
    pp["mla_q_norm_partner"] = _head_slot(_swap_rotary_halves(p["mla_q_norm"]))
    return pp


def _rope_tables(n):
    t = jnp.arange(n)
    row = (t // GRID_W).astype(F32)
    col = (t % GRID_W).astype(F32)
    axis_dim = ROPE // 2
    inv = 1.0 / (ROPE_BASE ** (jnp.arange(0, axis_dim, 2, dtype=F32) / axis_dim))
    ang = jnp.concatenate([row[:, None] * inv, col[:, None] * inv], axis=-1)
    cos, sin = jnp.cos(ang), jnp.sin(ang)
    tail = jnp.zeros((n, MLA_HP - QK_DIM), F32)
    rc = jnp.concatenate([jnp.ones((n, NOPE), F32), cos, cos, tail], axis=-1)
    rs = jnp.concatenate([jnp.zeros((n, NOPE), F32), -sin, sin, tail], axis=-1)
    return rc, rs


def _na_pair_bias(rpb):
    c = np.arange(GRID_W)
    start = np.clip(c - KW // 2, 0, GRID_W - KW)
    kc = np.arange(GRID_W)
    valid = (kc[None, :] >= start[:, None]) & (kc[None, :] < start[:, None] + KW)
    dc = kc[None, :] - c[:, None] + KW - 1
    onehot = ((dc[None] == np.arange(2 * KW - 1)[:, None, None]) & valid[None]).astype(np.float32)
    blocks = jnp.einsum("lhij,jck->lhick", rpb.astype(F32), onehot, precision=lax.Precision.HIGHEST)
    blocks = jnp.where(valid, blocks, NEG_INF)
    blocks = jnp.pad(blocks, ((0, 0), (0, 0), (1, 1), (0, 0), (0, 0)), constant_values=NEG_INF)
    return jnp.concatenate([blocks[:, :, :-1], blocks[:, :, 1:]], axis=-1).astype(F32)


def kernel(x_prompt, x_sample, cache_na_k, cache_na_v, cache_mla_ckv, cache_mla_kpe, c, c_ctx,
           ada_w, ada_b, norm1_g, norm2_g, mlp_w1, mlp_w2,
           even_w_in, even_conv_w, na_q_norm, na_k_norm, na_rpb, even_w_out,
           odd_w_in, pool_w, pool_scale, q_a_norm, w_q_b, kv_a_norm, w_kv_b,
           mla_q_norm, mla_k_norm, odd_w_out):
    bp, lp, _ = x_prompt.shape
    bs, ls, _ = x_sample.shape
    pp = _prepare(dict(norm1_g=norm1_g, norm2_g=norm2_g, mlp_w1=mlp_w1, mlp_w2=mlp_w2,
                       even_w_in=even_w_in, even_conv_w=even_conv_w, na_q_norm=na_q_norm,
                       na_k_norm=na_k_norm, even_w_out=even_w_out, odd_w_in=odd_w_in, pool_w=pool_w,
                       pool_scale=pool_scale, q_a_norm=q_a_norm, w_q_b=w_q_b, kv_a_norm=kv_a_norm,
                       w_kv_b=w_kv_b, mla_q_norm=mla_q_norm, mla_k_norm=mla_k_norm,
                       odd_w_out=odd_w_out))
    c_all = jnp.concatenate(
        [c_ctx[None, :], c, jnp.zeros((MOD_ROWS - 1 - bs, D), F32)], axis=0)
    mods = _ada_call(c_all, ada_w, ada_b[:, None, :]).reshape(DEPTH, MOD_ROWS, 1, 6 * D)
    rope_tabs = _rope_tables(ls)
    pair_bias = _na_pair_bias(na_rpb)
    past = cache_na_k.shape[2]
    cache_k = cache_na_k.reshape(bs, -1, past, NA_HEADS * NA_DH)
    cache_v = cache_na_v.reshape(bs, -1, past, NA_HEADS * NA_DH)
    cache_kpe_slot = jnp.pad(cache_mla_kpe, ((0, 0), (0, 0), (0, 0), (NOPE, MLA_HP - QK_DIM)))
    ctx_row = lambda b: 0
    lat_row = lambda b: b + 1

    yp, ys = x_prompt, x_sample
    na_k, na_v, mla_ckv, mla_kpe = [], [], [], []
    for l in range(DEPTH):
        i = l // 2
        if l % 2 == 0:
            ap, att_p, kt, vt = _even_in_call(yp, mods, l, i, ctx_row, pp, True)
            na_k.append(kt.reshape(bp, NA_HEADS, NA_DH, lp))
            na_v.append(vt.reshape(bp, NA_HEADS, NA_DH, lp))
            a_s, qs, k_s, vs = _even_in_call(ys, mods, l, i, lat_row, pp, False)
            att_s = _na_call(qs, k_s, vs, cache_k, cache_v, i, pair_bias)
            w_out = pp["even_w_out"]
        else:
            ap, att_p, ckvp, kpep = _odd_in_call(yp, mods, l, i, ctx_row, pp, None, True)
            mla_ckv.append(ckvp)
            mla_kpe.append(kpep[:, :, NOPE:QK_DIM])
            a_s, qs, k_s, vs = _odd_in_call(ys, mods, l, i, lat_row, pp, rope_tabs, False)
            kc, vc = _cache_kv_call(cache_mla_ckv, cache_kpe_slot, i, pp)
            att_s = _attn_call(qs, [(kc, vc), (k_s, vs)], MLA_HP, 2)
            w_out = pp["odd_w_out"]
        yp = _out_mlp_call(yp.reshape(1, bp * lp, D), ap.reshape(1, bp * lp, A_W),
                           att_p.reshape(1, bp * lp, A_W), mods, l, ctx_row, w_out, i,
                           pp).reshape(bp, lp, D)
        ys = _out_mlp_call(ys, a_s, att_s, mods, l, lat_row, w_out, i, pp)
    state_k = jnp.transpose(jnp.stack(na_k, axis=0), (1, 0, 4, 2, 3))
    state_v = jnp.transpose(jnp.stack(na_v, axis=0), (1, 0, 4, 2, 3))
    return (yp, ys, state_k, state_v, jnp.stack(mla_ckv, axis=1), jnp.stack(mla_kpe, axis=1))
```

```python
import functools

import jax
import jax.numpy as jnp
import numpy as np
from jax import lax
from jax.experimental import pallas as pl
from jax.experimental.pallas import tpu as pltpu

F32 = jnp.float32
BF16 = jnp.bfloat16

D = 1024
DEPTH = 4
GRID_W = 64
GRID_R = 16
A_W = 512
NA_HEADS = 8
NA_DH = 64
KH = 8
KW = 16
POOL_WINDOWS = (2, 4, 8, 16)
POOL_G = 128
POOL_PAD = 16
MLA_HEADS = 8
NOPE = 64
ROPE = 32
QK_DIM = NOPE + ROPE
MLA_HP = 128
V_DIM = 64
Q_LORA = 384
KV_LORA = 256
ROPE_BASE = 10000.0
MLP_HIDDEN = 4 * D
NEG_INF = -1e30
EPS = 1e-6
MOD_ROWS = 16
LANES = 128
NA_WIN_ROWS = 12
Q_CHUNK = 256
LATENT_Q_CHUNK = 512
VMEM_LIMIT = 56 * 1024 * 1024


def _dot(a, b):
    return jnp.dot(a, b, preferred_element_type=F32)


def _dot_nt(a, b):
    return lax.dot_general(a, b, (((1,), (1,)), ((), ())), preferred_element_type=F32)


def _rms(x, g):
    ms = jnp.mean(x * x, axis=-1, keepdims=True)
    return x * lax.rsqrt(ms + EPS) * g


def _modulated(x, g, shift, scale):
    return _rms(x, g) * (1.0 + scale) + shift


def _head_rms_64(x, g):
    t_rows = x.shape[0]
    low = lax.broadcasted_iota(jnp.int32, (t_rows, LANES), 1) < NA_DH
    cols = []
    for j in range(x.shape[1] // LANES):
        t = x[:, j * LANES:(j + 1) * LANES]
        t2 = t * t
        s_lo = jnp.sum(jnp.where(low, t2, 0.0), axis=-1, keepdims=True)
        s_hi = jnp.sum(jnp.where(low, 0.0, t2), axis=-1, keepdims=True)
        r_lo = lax.rsqrt(s_lo * (1.0 / NA_DH) + EPS)
        r_hi = lax.rsqrt(s_hi * (1.0 / NA_DH) + EPS)
        cols.append(t * jnp.where(low, r_lo, r_hi))
    return jnp.concatenate(cols, axis=-1) * g


def _slot_inv_rms(t):
    return lax.rsqrt(jnp.sum(t * t, axis=-1, keepdims=True) * (1.0 / QK_DIM) + EPS)


def _rotary_partner(t):
    x1_side = lax.broadcasted_iota(jnp.int32, t.shape, 1) < NOPE + ROPE // 2
    return jnp.where(x1_side, pltpu.roll(t, LANES - ROPE // 2, 1), pltpu.roll(t, ROPE // 2, 1))


def _mla_q(qf, q_partner, g_slot, g_partner, rope_tabs):
    if rope_tabs is not None:
        rc, rs = rope_tabs
        gc, gs = g_slot * rc, g_partner * rs
    cols = []
    for j in range(MLA_HEADS):
        t = qf[:, j * MLA_HP:(j + 1) * MLA_HP]
        if rope_tabs is None:
            u = t * g_slot
        else:
            u = t * gc + q_partner[:, j * MLA_HP:(j + 1) * MLA_HP] * gs
        cols.append(u * (_slot_inv_rms(t) * (QK_DIM ** -0.5)))
    return jnp.concatenate(cols, axis=-1)


def _kv_up(ckv, kpe_slot, wkvb_ref, g_slot, rope_tabs):
    kvb = _dot(ckv.astype(BF16), wkvb_ref[...])
    kr = kpe_slot * g_slot
    if rope_tabs is not None:
        rc, rs = rope_tabs
        kr = kr * rc + _rotary_partner(kr) * rs
    cols = []
    for j in range(MLA_HEADS):
        kn = kvb[:, j * MLA_HP:(j + 1) * MLA_HP]
        cols.append((kn * g_slot + kr) * _slot_inv_rms(kn + kpe_slot))
    return jnp.concatenate(cols, axis=-1), kvb[:, MLA_HEADS * MLA_HP:]


def _ada_kernel(c_ref, w_ref, b_ref, o_ref):
    c = c_ref[...]
    s = c * (1.0 / (1.0 + jnp.exp(-c)))
    o_ref[...] = _dot(s.astype(BF16), w_ref[...].astype(BF16)) + b_ref[...]


def _even_in_kernel(y_ref, mod_ref, g1_ref, w_ref, cw_ref, gq_ref, gk_ref, *out_refs, seq, context):
    if context:
        a_ref, att_ref, kt_ref, vt_ref, q_ref, k_ref, v_ref = out_refs
    else:
        a_ref, q_ref, k_ref, v_ref = out_refs
    mod = mod_ref[...]
    h = _modulated(y_ref[...], g1_ref[...], mod[:, 0:D], mod[:, D:2 * D])
    z = _dot(h.astype(BF16), w_ref[...])
    u = z[:, A_W:2 * A_W] * z[:, 2 * A_W:3 * A_W]
    t = lax.broadcasted_iota(jnp.int32, (seq, A_W), 0)
    prev = jnp.where(t > 0, pltpu.roll(u, 1, 0), 0.0)
    nxt = jnp.where(t < seq - 1, pltpu.roll(u, seq - 1, 0), 0.0)
    cw = cw_ref[...]
    a = z[:, 0:A_W] * (prev * cw[0:1] + u * cw[1:2] + nxt * cw[2:3])
    a_ref[...] = a.astype(BF16)
    q = _head_rms_64(z[:, 3 * A_W:4 * A_W], gq_ref[...]) * (NA_DH ** -0.5)
    q_ref[...] = q.astype(BF16)
    k = _head_rms_64(z[:, 4 * A_W:5 * A_W], gk_ref[...])
    v = z[:, 5 * A_W:6 * A_W]
    k_ref[...] = k.astype(BF16)
    v_ref[...] = v.astype(BF16)
    if context:
        kt_ref[...] = k.T
        vt_ref[...] = v.T
        _attend_all(q_ref, [k_ref, v_ref], att_ref, seq, NA_DH, NA_HEADS // 2)


def _pool_branch(u, pw_ref, ps, seq):
    zpad = jnp.zeros((POOL_PAD, POOL_G), F32)
    padded = seq + 2 * POOL_PAD
    t = lax.broadcasted_iota(jnp.int32, (seq, POOL_G), 0)
    cols = []
    for g, w in enumerate(POOL_WINDOWS):
        ug = u[:, g * POOL_G:(g + 1) * POOL_G]
        xp = jnp.concatenate([zpad, ug, zpad], axis=0)
        acc = xp + pltpu.roll(xp, 1, 0)
        s = 1
        while 4 * s <= w:
            acc = pltpu.roll(acc, s, 0) + pltpu.roll(acc, padded - s, 0)
            s *= 2
        win = acc[POOL_PAD:POOL_PAD + seq]
        lo = jnp.maximum(t - w // 2, 0)
        hi = jnp.minimum(t - w // 2 + w, seq)
        pooled = win / (hi - lo).astype(F32)
        cols.append(_dot((pooled - ug).astype(BF16), pw_ref[g]))
    return jnp.concatenate(cols, axis=-1) * ps


def _odd_in_kernel(*refs, seq, rope, context):
    (y_ref, mod_ref, g1_ref, w_ref, pw_ref, ps_ref, qan_ref, wqb_ref, kvan_ref, wkvb_ref,
     gq_ref, gk_ref) = refs[:12]
    pos = 12
    rope_tabs = g_partner = q_partner = None
    if rope:
        rope_tabs = (refs[pos][...], refs[pos + 1][...])
        g_partner = refs[pos + 2][...]
        pos += 3
    if context:
        p_ref, att_ref, ckv_ref, kpe_ref, q_ref, k_ref, v_ref = refs[pos:pos + 7]
    else:
        p_ref, q_ref, k_ref, v_ref = refs[pos:pos + 4]
    mod = mod_ref[...]
    h = _modulated(y_ref[...], g1_ref[...], mod[:, 0:D], mod[:, D:2 * D])
    z = _dot(h.astype(BF16), w_ref[...])
    p_ref[...] = _pool_branch(z[:, 0:A_W], pw_ref, ps_ref[...], seq).astype(BF16)
    o_q, o_kv, o_pe = A_W, A_W + Q_LORA, A_W + Q_LORA + KV_LORA
    ql = _rms(z[:, o_q:o_kv], qan_ref[...])
    qf = _dot(ql.astype(BF16), wqb_ref[...])
    if rope:
        q_partner = qf[:, MLA_HEADS * MLA_HP:]
    q_ref[...] = _mla_q(qf, q_partner, gq_ref[...], g_partner, rope_tabs).astype(BF16)
    ckv = _rms(z[:, o_kv:o_pe], kvan_ref[...])
    kpe_slot = z[:, o_pe:o_pe + MLA_HP]
    k, v = _kv_up(ckv, kpe_slot, wkvb_ref, gk_ref[...], rope_tabs)
    k_ref[...] = k.astype(BF16)
    v_ref[...] = v.astype(BF16)
    if context:
        ckv_ref[...] = ckv
        kpe_ref[...] = kpe_slot
        _attend_all(q_ref, [k_ref, v_ref], att_ref, seq, MLA_HP, MLA_HEADS // 2)


def _cache_kv_kernel(ckv_ref, kpe_ref, wkvb_ref, gk_ref, k_ref, v_ref):
    k, v = _kv_up(ckv_ref[...], kpe_ref[...], wkvb_ref, gk_ref[...], None)
    k_ref[...] = k.astype(BF16)
    v_ref[...] = v.astype(BF16)


def _attend(q, sources):
    scores = []
    for k, _, bias in sources:
        s = _dot_nt(q, k)
        scores.append(s if bias is None else s + bias)
    m = scores[0].max(axis=-1, keepdims=True)
    for s in scores[1:]:
        m = jnp.maximum(m, s.max(axis=-1, keepdims=True))
    den = None
    out = None
    for s, (_, v, _) in zip(scores, sources):
        p = jnp.exp(s - m)
        ps = p.sum(axis=-1, keepdims=True)
        pv = _dot(p.astype(BF16), v)
        den = ps if den is None else den + ps
        out = pv if out is None else out + pv
    return out / den


def _pair_outputs(q_ref, q_rows, pair, dk, srcs, n_rows):
    low = lax.broadcasted_iota(jnp.int32, (n_rows, LANES), 1) < V_DIM
    vcols = slice(pair * LANES, (pair + 1) * LANES)
    outs = []
    for hh in range(2):
        if dk == NA_DH:
            q128 = q_ref[q_rows, pair * LANES:(pair + 1) * LANES]
            keep = low if hh == 0 else jnp.logical_not(low)
            q = jnp.where(keep, q128, jnp.zeros_like(q128))
            kcols = slice(pair * LANES, (pair + 1) * LANES)
        else:
            kcols = slice((2 * pair + hh) * dk, (2 * pair + hh + 1) * dk)
            q = q_ref[q_rows, kcols]
        sources = []
        for k_ref, k_rows, v_ref, v_rows, bias_pair in srcs:
            k = k_ref[k_rows, kcols].astype(BF16)
            v = v_ref[v_rows, vcols].astype(BF16)
            sources.append((k, v, None if bias_pair is None else bias_pair[hh]))
        outs.append(_attend(q, sources))
    return jnp.where(low, outs[0], outs[1])


def _attend_all(q_ref, kv, o_ref, lq, dk, n_pairs, q_chunk=Q_CHUNK):
    full = slice(None)
    for c in range(lq // q_chunk):
        rows = slice(c * q_chunk, (c + 1) * q_chunk)
        for pair in range(n_pairs):
            srcs = [(kv[2 * i], full, kv[2 * i + 1], full, None) for i in range(len(kv) // 2)]
            o = _pair_outputs(q_ref, rows, pair, dk, srcs, q_chunk)
            o_ref[rows, pair * LANES:(pair + 1) * LANES] = o.astype(o_ref.dtype)


def _attn_kernel(*refs, lq, dk, n_pairs, n_src, q_chunk):
    _attend_all(refs[0], refs[1:1 + 2 * n_src], refs[1 + 2 * n_src], lq, dk, n_pairs, q_chunk)


def _na_kernel(q_ref, k_ref, v_ref, ck_ref, cv_ref, pb_ref, o_ref, bias_ref, *, n_pairs):
    @pl.when(pl.program_id(1) == 0)
    def _():
        low = lax.broadcasted_iota(jnp.int32, (GRID_W, LANES), 1) < GRID_W
        neg = jnp.full((GRID_W, LANES), NEG_INF, F32)
        for hh in range(2 * n_pairs):
            for r in range(GRID_R):
                win0 = (r // 8) * 4
                rs = min(max(r - KH // 2, 0), GRID_R - KH)
                delta, dr0 = rs - win0, rs - r + KH - 1
                for p in range(NA_WIN_ROWS // 2):
                    ok0 = delta <= 2 * p < delta + KH
                    ok1 = delta <= 2 * p + 1 < delta + KH
                    if not (ok0 or ok1):
                        blk = neg
                    else:
                        blk = pb_ref[hh, dr0 + 2 * p - delta + 1]
                        if not ok0:
                            blk = jnp.where(low, NEG_INF, blk)
                        if not ok1:
                            blk = jnp.where(low, blk, NEG_INF)
                    bias_ref[hh, r * GRID_W:(r + 1) * GRID_W, p * LANES:(p + 1) * LANES] = blk

    full = slice(None)
    for c in range(GRID_R * GRID_W // Q_CHUNK):
        rows = slice(c * Q_CHUNK, (c + 1) * Q_CHUNK)
        row_lo = min(max(4 * c - KH // 2, 0), GRID_R - KH)
        row_hi = min(max(4 * c + 3 - KH // 2, 0), GRID_R - KH) + KH
        row_hi += (row_hi - row_lo) % 2
        win = slice(row_lo * GRID_W, row_hi * GRID_W)
        strip = slice((row_lo - (c // 2) * 4) * GRID_W, (row_hi - (c // 2) * 4) * GRID_W)
        for pair in range(n_pairs):
            bias_pair = (bias_ref[2 * pair, rows, strip], bias_ref[2 * pair + 1, rows, strip])
            srcs = [(k_ref, win, v_ref, win, bias_pair), (ck_ref, full, cv_ref, full, None)]
            o = _pair_outputs(q_ref, rows, pair, NA_DH, srcs, Q_CHUNK)
            o_ref[rows, pair * LANES:(pair + 1) * LANES] = o.astype(o_ref.dtype)


def _out_mlp_kernel(y_ref, b1_ref, b2_ref, mod_ref, wo_ref, g2_ref, w1_ref, w2_ref, o_ref, *, hc):
    mod = mod_ref[...]
    mix = _dot(b1_ref[...], wo_ref[0:A_W, :]) + _dot(b2_ref[...], wo_ref[A_W:2 * A_W, :])
    y1 = y_ref[...] + mod[:, 2 * D:3 * D] * mix
    h2 = _modulated(y1, g2_ref[...], mod[:, 3 * D:4 * D], mod[:, 4 * D:5 * D]).astype(BF16)
    acc = jnp.zeros(y1.shape, F32)
    for c in range(MLP_HIDDEN // hc):
        a = jnp.maximum(_dot(h2, w1_ref[:, c * hc:(c + 1) * hc]), 0.0)
        acc = acc + _dot((a * a).astype(BF16), w2_ref[c * hc:(c + 1) * hc, :])
    o_ref[...] = y1 + mod[:, 5 * D:6 * D] * acc


def _params(n_grid):
    return pltpu.CompilerParams(dimension_semantics=("arbitrary",) * n_grid,
                                vmem_limit_bytes=VMEM_LIMIT)


def _layer_spec(shape, layer):
    nd = len(shape)
    return pl.BlockSpec((None,) + tuple(shape[1:]), lambda *_: (layer,) + (0,) * (nd - 1),
                        pipeline_mode=pl.Buffered(1))


def _mod_spec(layer, row_of_batch):
    return pl.BlockSpec((None, None, 1, 6 * D), lambda b, *_: (layer, row_of_batch(b), 0, 0))


def _ada_call(c_all, ada_w, ada_b):
    tn = 1536
    return pl.pallas_call(
        _ada_kernel,
        out_shape=jax.ShapeDtypeStruct((DEPTH, MOD_ROWS, 6 * D), F32),
        grid=(DEPTH, 6 * D // tn),
        in_specs=[pl.BlockSpec((MOD_ROWS, D), lambda l, j: (0, 0)),
                  pl.BlockSpec((None, D, tn), lambda l, j: (l, 0, j)),
                  pl.BlockSpec((None, 1, tn), lambda l, j: (l, 0, j))],
        out_specs=pl.BlockSpec((None, MOD_ROWS, tn), lambda l, j: (l, 0, j)),
        compiler_params=_params(2), name="ada_mod",
    )(c_all, ada_w, ada_b)


def _even_in_call(y, mods, layer, i, row_of_batch, pp, context):
    nb, seq, _ = y.shape
    tok = lambda w: pl.BlockSpec((None, seq, w), lambda b: (b, 0, 0))
    tok_t = pl.BlockSpec((None, A_W, seq), lambda b: (b, 0, 0))
    state = jax.ShapeDtypeStruct((nb, A_W, seq), F32)
    act = jax.ShapeDtypeStruct((nb, seq, A_W), BF16)
    return pl.pallas_call(
        functools.partial(_even_in_kernel, seq=seq, context=context),
        out_shape=(act, act, state, state) if context else (act,) * 4,
        grid=(nb,),
        scratch_shapes=[pltpu.VMEM((seq, A_W), BF16)] * 3 if context else [],
        in_specs=[tok(D), _mod_spec(layer, row_of_batch), _layer_spec(pp["norm1_g"].shape, layer),
                  _layer_spec(pp["even_w_in"].shape, i), _layer_spec(pp["even_conv_w"].shape, i),
                  _layer_spec(pp["na_q_norm"].shape, i), _layer_spec(pp["na_k_norm"].shape, i)],
        out_specs=(tok(A_W), tok(A_W), tok_t, tok_t) if context else (tok(A_W),) * 4,
        compiler_params=_params(1), name="even_in",
    )(y, mods, pp["norm1_g"], pp["even_w_in"], pp["even_conv_w"], pp["na_q_norm"], pp["na_k_norm"])


def _odd_in_call(y, mods, layer, i, row_of_batch, pp, rope_tabs, context):
    nb, seq, _ = y.shape
    tok = lambda w: pl.BlockSpec((None, seq, w), lambda b: (b, 0, 0))
    rope = rope_tabs is not None
    names = ["norm1_g", "odd_w_in", "pool_w", "pool_scale", "q_a_norm",
             "w_q_b_rope" if rope else "w_q_b", "kv_a_norm", "w_kv_b", "mla_q_norm", "mla_k_norm"]
    layers = [layer] + [i] * 9
    in_specs = [tok(D), _mod_spec(layer, row_of_batch)]
    in_specs += [_layer_spec(pp[n].shape, li) for n, li in zip(names, layers)]
    args = [y, mods] + [pp[n] for n in names]
    if rope:
        in_specs += [pl.BlockSpec((seq, MLA_HP), lambda b: (0, 0))] * 2
        in_specs += [_layer_spec(pp["mla_q_norm_partner"].shape, i)]
        args += list(rope_tabs) + [pp["mla_q_norm_partner"]]
    qk_w, v_w = MLA_HEADS * MLA_HP, MLA_HEADS * V_DIM
    if context:
        widths, dtypes = [A_W, v_w, KV_LORA, MLA_HP], [BF16, BF16, F32, F32]
        scratch = [pltpu.VMEM((seq, qk_w), BF16), pltpu.VMEM((seq, qk_w), BF16),
                   pltpu.VMEM((seq, v_w), BF16)]
    else:
        widths, dtypes = [A_W, qk_w, qk_w, v_w], [BF16] * 4
        scratch = []
    return pl.pallas_call(
        functools.partial(_odd_in_kernel, seq=seq, rope=rope, context=context),
        out_shape=tuple(jax.ShapeDtypeStruct((nb, seq, w), dt) for w, dt in zip(widths, dtypes)),
        grid=(nb,), in_specs=in_specs, out_specs=tuple(tok(w) for w in widths),
        scratch_shapes=scratch, compiler_params=_params(1), name="odd_in",
    )(*args)


def _cache_kv_call(cache_ckv, cache_kpe_slot, i, pp):
    nb, _, past, _ = cache_ckv.shape
    cache = lambda w: pl.BlockSpec((None, None, past, w), lambda b: (b, i, 0, 0))
    tok = lambda w: pl.BlockSpec((None, past, w), lambda b: (b, 0, 0))
    return pl.pallas_call(
        _cache_kv_kernel,
        out_shape=(jax.ShapeDtypeStruct((nb, past, MLA_HEADS * MLA_HP), BF16),
                   jax.ShapeDtypeStruct((nb, past, MLA_HEADS * V_DIM), BF16)),
        grid=(nb,),
        in_specs=[cache(KV_LORA), cache(MLA_HP), _layer_spec(pp["w_kv_b"].shape, i),
                  _layer_spec(pp["mla_k_norm"].shape, i)],
        out_specs=(tok(MLA_HEADS * MLA_HP), tok(MLA_HEADS * V_DIM)),
        compiler_params=_params(1), name="cache_kv",
    )(cache_ckv, cache_kpe_slot, pp["w_kv_b"], pp["mla_k_norm"])


def _attn_call(q, kvs, dk, pairs_per_step):
    nb, lq, _ = q.shape
    n_groups = NA_HEADS // (2 * pairs_per_step)
    qk_w, v_w = 2 * dk * pairs_per_step, LANES * pairs_per_step
    in_specs = [pl.BlockSpec((None, lq, qk_w), lambda g, b: (b, 0, g))]
    args = [q]
    for k, v in kvs:
        in_specs += [pl.BlockSpec((None, k.shape[1], qk_w), lambda g, b: (b, 0, g)),
                     pl.BlockSpec((None, v.shape[1], v_w), lambda g, b: (b, 0, g))]
        args += [k, v]
    return pl.pallas_call(
        functools.partial(_attn_kernel, lq=lq, dk=dk, n_pairs=pairs_per_step, n_src=len(kvs),
                          q_chunk=LATENT_Q_CHUNK),
        out_shape=jax.ShapeDtypeStruct((nb, lq, NA_HEADS * V_DIM), BF16),
        grid=(n_groups, nb), in_specs=in_specs,
        out_specs=pl.BlockSpec((None, lq, v_w), lambda g, b: (b, 0, g)),
        compiler_params=_params(2), name="attn",
    )(*args)


def _na_call(q, k, v, cache_k, cache_v, i, pair_bias):
    nb, lq, _ = q.shape
    past = cache_k.shape[2]
    n_pairs = 2
    width = n_pairs * LANES
    tok = pl.BlockSpec((None, lq, width), lambda g, b: (b, 0, g))
    cache = pl.BlockSpec((None, None, past, width), lambda g, b: (b, i, 0, g))
    return pl.pallas_call(
        functools.partial(_na_kernel, n_pairs=n_pairs),
        out_shape=jax.ShapeDtypeStruct((nb, lq, NA_HEADS * NA_DH), BF16),
        grid=(NA_HEADS // (2 * n_pairs), nb),
        in_specs=[tok, tok, tok, cache, cache,
                  pl.BlockSpec((None, 2 * n_pairs, 2 * KH, GRID_W, LANES),
                               lambda g, b: (i, g, 0, 0, 0))],
        out_specs=tok,
        scratch_shapes=[pltpu.VMEM((2 * n_pairs, lq, NA_WIN_ROWS * GRID_W), F32)],
        compiler_params=_params(2), name="na_attn",
    )(q, k, v, cache_k, cache_v, pair_bias)


def _out_mlp_call(y, b1, b2, mods, layer, row_of_batch, w_out, w_out_layer, pp):
    nb, seq, _ = y.shape
    tm = 1024
    tok = lambda w: pl.BlockSpec((None, tm, w), lambda b, t: (b, t, 0))
    return pl.pallas_call(
        functools.partial(_out_mlp_kernel, hc=512),
        out_shape=jax.ShapeDtypeStruct(y.shape, F32),
        grid=(nb, seq // tm),
        in_specs=[tok(D), tok(A_W), tok(A_W), _mod_spec(layer, row_of_batch),
                  _layer_spec(w_out.shape, w_out_layer), _layer_spec(pp["norm2_g"].shape, layer),
                  _layer_spec(pp["mlp_w1"].shape, layer), _layer_spec(pp["mlp_w2"].shape, layer)],
        out_specs=tok(D),
        compiler_params=_params(2), name="out_mlp",
    )(y, b1, b2, mods, w_out, pp["norm2_g"], pp["mlp_w1"], pp["mlp_w2"])


def _row(v):
    return v[:, None, :].astype(F32)


def _head_slot(v):
    return _row(jnp.pad(v, ((0, 0), (0, MLA_HP - QK_DIM))))


def _swap_rotary_halves(v):
    half = ROPE // 2
    return jnp.concatenate(
        [jnp.zeros_like(v[..., :NOPE]), v[..., NOPE + half:], v[..., NOPE:NOPE + half]], axis=-1)


def _prepare(p):
    n_odd = p["odd_w_in"].shape[0]
    pp = {}
    pp["norm1_g"], pp["norm2_g"] = _row(p["norm1_g"]), _row(p["norm2_g"])
    pp["mlp_w1"], pp["mlp_w2"] = p["mlp_w1"].astype(BF16), p["mlp_w2"].astype(BF16)
    pp["even_w_in"] = p["even_w_in"].astype(BF16)
    pp["even_conv_w"] = p["even_conv_w"].astype(F32)
    pp["na_q_norm"] = _row(jnp.tile(p["na_q_norm"], (1, NA_HEADS)))
    pp["na_k_norm"] = _row(jnp.tile(p["na_k_norm"], (1, NA_HEADS)))
    pp["even_w_out"] = p["even_w_out"].astype(BF16)
    pp["odd_w_out"] = p["odd_w_out"].astype(BF16)
    w = p["odd_w_in"]
    o_pe = A_W + Q_LORA + KV_LORA
    pe = jnp.pad(w[:, :, o_pe:], ((0, 0), (0, 0), (NOPE, MLA_HP - QK_DIM)))
    pp["odd_w_in"] = jnp.concatenate([w[:, :, :o_pe], pe], axis=-1).astype(BF16)
    pp["pool_w"] = p["pool_w"].astype(BF16)
    pp["pool_scale"] = _row(p["pool_scale"])
    pp["q_a_norm"], pp["kv_a_norm"] = _row(p["q_a_norm"]), _row(p["kv_a_norm"])
    slot_pad = ((0, 0), (0, 0), (0, 0), (0, MLA_HP - QK_DIM))
    wq = p["w_q_b"].reshape(n_odd, Q_LORA, MLA_HEADS, QK_DIM)
    wq_slots = jnp.pad(wq, slot_pad).reshape(n_odd, Q_LORA, MLA_HEADS * MLA_HP)
    wq_partner = jnp.pad(_swap_rotary_halves(wq), slot_pad).reshape(n_odd, Q_LORA, MLA_HEADS * MLA_HP)
    pp["w_q_b"] = wq_slots.astype(BF16)
    pp["w_q_b_rope"] = jnp.concatenate([wq_slots, wq_partner], axis=-1).astype(BF16)
    wkv = p["w_kv_b"].reshape(n_odd, KV_LORA, MLA_HEADS, NOPE + V_DIM)
    wk = jnp.pad(wkv[..., :NOPE], ((0, 0), (0, 0), (0, 0), (0, MLA_HP - NOPE)))
    pp["w_kv_b"] = jnp.concatenate(
        [wk.reshape(n_odd, KV_LORA, MLA_HEADS * MLA_HP),
         wkv[..., NOPE:].reshape(n_odd, KV_LORA, MLA_HEADS * V_DIM)], axis=-1).astype(BF16)
    pp["mla_q_norm"], pp["mla_k_norm"] = _head_slot(p["mla_q_norm"]), _head_slot(p["mla_k_norm"])
    pp["mla_q_norm_partner"] = _head_slot(_swap_rotary_halves(p["mla_q_norm"]))
    return pp


def _rope_tables(n):
    t = jnp.arange(n)
    row = (t // GRID_W).astype(F32)
    col = (t % GRID_W).astype(F32)
    axis_dim = ROPE // 2
    inv = 1.0 / (ROPE_BASE ** (jnp.arange(0, axis_dim, 2, dtype=F32) / axis_dim))
    ang = jnp.concatenate([row[:, None] * inv, col[:, None] * inv], axis=-1)
    cos, sin = jnp.cos(ang), jnp.sin(ang)
    tail = jnp.zeros((n, MLA_HP - QK_DIM), F32)
    rc = jnp.concatenate([jnp.ones((n, NOPE), F32), cos, cos, tail], axis=-1)
    rs = jnp.concatenate([jnp.zeros((n, NOPE), F32), -sin, sin, tail], axis=-1)
    return rc, rs


def _na_pair_bias(rpb):
    c = np.arange(GRID_W)
    start = np.clip(c - KW // 2, 0, GRID_W - KW)
    kc = np.arange(GRID_W)
    valid = (kc[None, :] >= start[:, None]) & (kc[None, :] < start[:, None] + KW)
    dc = kc[None, :] - c[:, None] + KW - 1
    onehot = ((dc[None] == np.arange(2 * KW - 1)[:, None, None]) & valid[None]).astype(np.float32)
    blocks = jnp.einsum("lhij,jck->lhick", rpb.astype(F32), onehot, precision=lax.Precision.HIGHEST)
    blocks = jnp.where(valid, blocks, NEG_INF)
    blocks = jnp.pad(blocks, ((0, 0), (0, 0), (1, 1), (0, 0), (0, 0)), constant_values=NEG_INF)
    return jnp.concatenate([blocks[:, :, :-1], blocks[:, :, 1:]], axis=-1).astype(F32)


def kernel(x_prompt, x_sample, cache_na_k, cache_na_v, cache_mla_ckv, cache_mla_kpe, c, c_ctx,
           ada_w, ada_b, norm1_g, norm2_g, mlp_w1, mlp_w2,
           even_w_in, even_conv_w, na_q_norm, na_k_norm, na_rpb, even_w_out,
           odd_w_in, pool_w, pool_scale, q_a_norm, w_q_b, kv_a_norm, w_kv_b,
           mla_q_norm, mla_k_norm, odd_w_out):
    bp, lp, _ = x_prompt.shape
    bs, ls, _ = x_sample.shape
    pp = _prepare(dict(norm1_g=norm1_g, norm2_g=norm2_g, mlp_w1=mlp_w1, mlp_w2=mlp_w2,
                       even_w_in=even_w_in, even_conv_w=even_conv_w, na_q_norm=na_q_norm,
                       na_k_norm=na_k_norm, even_w_out=even_w_out, odd_w_in=odd_w_in, pool_w=pool_w,
                       pool_scale=pool_scale, q_a_norm=q_a_norm, w_q_b=w_q_b, kv_a_norm=kv_a_norm,
                       w_kv_b=w_kv_b, mla_q_norm=mla_q_norm, mla_k_norm=mla_k_norm,
                       odd_w_out=odd_w_out))
    c_all = jnp.concatenate(
        [c_ctx[None, :], c, jnp.zeros((MOD_ROWS - 1 - bs, D), F32)], axis=0)
    mods = _ada_call(c_all, ada_w, ada_b[:, None, :]).reshape(DEPTH, MOD_ROWS, 1, 6 * D)
    rope_tabs = _rope_tables(ls)
    pair_bias = _na_pair_bias(na_rpb)
    past = cache_na_k.shape[2]
    cache_k = cache_na_k.reshape(bs, -1, past, NA_HEADS * NA_DH)
    cache_v = cache_na_v.reshape(bs, -1, past, NA_HEADS * NA_DH)
    cache_kpe_slot = jnp.pad(cache_mla_kpe, ((0, 0), (0, 0), (0, 0), (NOPE, MLA_HP - QK_DIM)))
    ctx_row = lambda b: 0
    lat_row = lambda b: b + 1

    yp, ys = x_prompt, x_sample
    na_k, na_v, mla_ckv, mla_kpe = [], [], [], []
    for l in range(DEPTH):
        i = l // 2
        if l % 2 == 0:
            ap, att_p, kt, vt = _even_in_call(yp, mods, l, i, ctx_row, pp, True)
            na_k.append(kt.reshape(bp, NA_HEADS, NA_DH, lp))
            na_v.append(vt.reshape(bp, NA_HEADS, NA_DH, lp))
            a_s, qs, k_s, vs = _even_in_call(ys, mods, l, i, lat_row, pp, False)
            att_s = _na_call(qs, k_s, vs, cache_k, cache_v, i, pair_bias)
            w_out = pp["even_w_out"]
        else:
            ap, att_p, ckvp, kpep = _odd_in_call(yp, mods, l, i, ctx_row, pp, None, True)
            mla_ckv.append(ckvp)
            mla_kpe.append(kpep[:, :, NOPE:QK_DIM])
            a_s, qs, k_s, vs = _odd_in_call(ys, mods, l, i, lat_row, pp, rope_tabs, False)
            kc, vc = _cache_kv_call(cache_mla_ckv, cache_kpe_slot, i, pp)
            att_s = _attn_call(qs, [(kc, vc), (k_s, vs)], MLA_HP, 2)
            w_out = pp["odd_w_out"]
        yp = _out_mlp_call(yp.reshape(1, bp * lp, D), ap.reshape(1, bp * lp, A_W),
                           att_p.reshape(1, bp * lp, A_W), mods, l, ctx_row, w_out, i,
                           pp).reshape(bp, lp, D)
        ys = _out_mlp_call(ys, a_s, att_s, mods, l, lat_row, w_out, i, pp)
    state_k = jnp.transpose(jnp.stack(na_k, axis=0), (1, 0, 4, 2, 3))
    state_v = jnp.transpose(jnp.stack(na_v, axis=0), (1, 0, 4, 2, 3))
    return (yp, ys, state_k, state_v, jnp.stack(mla_ckv, axis=1), jnp.stack(mla_kpe, axis=1))
```

```python
import functools

import jax
import jax.numpy as jnp
import numpy as np
from jax import lax
from jax.experimental import pallas as pl
from jax.experimental.pallas import tpu as pltpu

F32 = jnp.float32
BF16 = jnp.bfloat16

D = 1024
DEPTH = 4
GRID_W = 64
GRID_R = 16
A_W = 512
NA_HEADS = 8
NA_DH = 64
KH = 8
KW = 16
POOL_WINDOWS = (2, 4, 8, 16)
POOL_G = 128
POOL_PAD = 16
MLA_HEADS = 8
NOPE = 64
ROPE = 32
QK_DIM = NOPE + ROPE
MLA_HP = 128
V_DIM = 64
Q_LORA = 384
KV_LORA = 256
ROPE_BASE = 10000.0
MLP_HIDDEN = 4 * D
NEG_INF = -1e30
EPS = 1e-6
MOD_ROWS = 16
LANES = 128
NA_WIN_ROWS = 12
Q_CHUNK = 256
LATENT_Q_CHUNK = 512
VMEM_LIMIT = 56 * 1024 * 1024


def _dot(a, b):
    return jnp.dot(a, b, preferred_element_type=F32)


def _dot_nt(a, b):
    return lax.dot_general(a, b, (((1,), (1,)), ((), ())), preferred_element_type=F32)


def _rms(x, g):
    ms = jnp.mean(x * x, axis=-1, keepdims=True)
    return x * lax.rsqrt(ms + EPS) * g


def _modulated(x, g, shift, scale):
    return _rms(x, g) * (1.0 + scale) + shift


def _head_rms_64(x, g):
    t_rows = x.shape[0]
    low = lax.broadcasted_iota(jnp.int32, (t_rows, LANES), 1) < NA_DH
    cols = []
    for j in range(x.shape[1] // LANES):
        t = x[:, j * LANES:(j + 1) * LANES]
        t2 = t * t
        s_lo = jnp.sum(jnp.where(low, t2, 0.0), axis=-1, keepdims=True)
        s_hi = jnp.sum(jnp.where(low, 0.0, t2), axis=-1, keepdims=True)
        r_lo = lax.rsqrt(s_lo * (1.0 / NA_DH) + EPS)
        r_hi = lax.rsqrt(s_hi * (1.0 / NA_DH) + EPS)
        cols.append(t * jnp.where(low, r_lo, r_hi))
    return jnp.concatenate(cols, axis=-1) * g


def _slot_inv_rms(t):
    return lax.rsqrt(jnp.sum(t * t, axis=-1, keepdims=True) * (1.0 / QK_DIM) + EPS)


def _rotary_partner(t):
    x1_side = lax.broadcasted_iota(jnp.int32, t.shape, 1) < NOPE + ROPE // 2
    return jnp.where(x1_side, pltpu.roll(t, LANES - ROPE // 2, 1), pltpu.roll(t, ROPE // 2, 1))


def _mla_q(qf, q_partner, g_slot, g_partner, rope_tabs):
    if rope_tabs is not None:
        rc, rs = rope_tabs
        gc, gs = g_slot * rc, g_partner * rs
    cols = []
    for j in range(MLA_HEADS):
        t = qf[:, j * MLA_HP:(j + 1) * MLA_HP]
        if rope_tabs is None:
            u = t * g_slot
        else:
            u = t * gc + q_partner[:, j * MLA_HP:(j + 1) * MLA_HP] * gs
        cols.append(u * (_slot_inv_rms(t) * (QK_DIM ** -0.5)))
    return jnp.concatenate(cols, axis=-1)


def _kv_up(ckv, kpe_slot, wkvb_ref, g_slot, rope_tabs):
    kvb = _dot(ckv.astype(BF16), wkvb_ref[...])
    kr = kpe_slot * g_slot
    if rope_tabs is not None:
        rc, rs = rope_tabs
        kr = kr * rc + _rotary_partner(kr) * rs
    cols = []
    for j in range(MLA_HEADS):
        kn = kvb[:, j * MLA_HP:(j + 1) * MLA_HP]
        cols.append((kn * g_slot + kr) * _slot_inv_rms(kn + kpe_slot))
    return jnp.concatenate(cols, axis=-1), kvb[:, MLA_HEADS * MLA_HP:]


def _ada_kernel(c_ref, w_ref, b_ref, o_ref):
    c = c_ref[...]
    s = c * (1.0 / (1.0 + jnp.exp(-c)))
    o_ref[...] = _dot(s.astype(BF16), w_ref[...].astype(BF16)) + b_ref[...]


def _even_in_kernel(y_ref, mod_ref, g1_ref, w_ref, cw_ref, gq_ref, gk_ref, *out_refs, seq, context):
    if context:
        a_ref, att_ref, kt_ref, vt_ref, q_ref, k_ref, v_ref = out_refs
    else:
        a_ref, q_ref, k_ref, v_ref = out_refs
    mod = mod_ref[...]
    h = _modulated(y_ref[...], g1_ref[...], mod[:, 0:D], mod[:, D:2 * D])
    z = _dot(h.astype(BF16), w_ref[...])
    u = z[:, A_W:2 * A_W] * z[:, 2 * A_W:3 * A_W]
    t = lax.broadcasted_iota(jnp.int32, (seq, A_W), 0)
    prev = jnp.where(t > 0, pltpu.roll(u, 1, 0), 0.0)
    nxt = jnp.where(t < seq - 1, pltpu.roll(u, seq - 1, 0), 0.0)
    cw = cw_ref[...]
    a = z[:, 0:A_W] * (prev * cw[0:1] + u * cw[1:2] + nxt * cw[2:3])
    a_ref[...] = a.astype(BF16)
    q = _head_rms_64(z[:, 3 * A_W:4 * A_W], gq_ref[...]) * (NA_DH ** -0.5)
    q_ref[...] = q.astype(BF16)
    k = _head_rms_64(z[:, 4 * A_W:5 * A_W], gk_ref[...])
    v = z[:, 5 * A_W:6 * A_W]
    k_ref[...] = k.astype(BF16)
    v_ref[...] = v.astype(BF16)
    if context:
        kt_ref[...] = k.T
        vt_ref[...] = v.T
        _attend_all(q_ref, [k_ref, v_ref], att_ref, seq, NA_DH, NA_HEADS // 2)


def _pool_branch(u, pw_ref, ps, seq):
    zpad = jnp.zeros((POOL_PAD, POOL_G), F32)
    padded = seq + 2 * POOL_PAD
    t = lax.broadcasted_iota(jnp.int32, (seq, POOL_G), 0)
    cols = []
    for g, w in enumerate(POOL_WINDOWS):
        ug = u[:, g * POOL_G:(g + 1) * POOL_G]
        xp = jnp.concatenate([zpad, ug, zpad], axis=0)
        acc = xp + pltpu.roll(xp, 1, 0)
        s = 1
        while 4 * s <= w:
            acc = pltpu.roll(acc, s, 0) + pltpu.roll(acc, padded - s, 0)
            s *= 2
        win = acc[POOL_PAD:POOL_PAD + seq]
        lo = jnp.maximum(t - w // 2, 0)
        hi = jnp.minimum(t - w // 2 + w, seq)
        pooled = win / (hi - lo).astype(F32)
        cols.append(_dot((pooled - ug).astype(BF16), pw_ref[g]))
    return jnp.concatenate(cols, axis=-1) * ps


def _odd_in_kernel(*refs, seq, rope, context):
    (y_ref, mod_ref, g1_ref, w_ref, pw_ref, ps_ref, qan_ref, wqb_ref, kvan_ref, wkvb_ref,
     gq_ref, gk_ref) = refs[:12]
    pos = 12
    rope_tabs = g_partner = q_partner = None
    if rope:
        rope_tabs = (refs[pos][...], refs[pos + 1][...])
        g_partner = refs[pos + 2][...]
        pos += 3
    if context:
        p_ref, att_ref, ckv_ref, kpe_ref, q_ref, k_ref, v_ref = refs[pos:pos + 7]
    else:
        p_ref, q_ref, k_ref, v_ref = refs[pos:pos + 4]
    mod = mod_ref[...]
    h = _modulated(y_ref[...], g1_ref[...], mod[:, 0:D], mod[:, D:2 * D])
    z = _dot(h.astype(BF16), w_ref[...])
    p_ref[...] = _pool_branch(z[:, 0:A_W], pw_ref, ps_ref[...], seq).astype(BF16)
    o_q, o_kv, o_pe = A_W, A_W + Q_LORA, A_W + Q_LORA + KV_LORA
    ql = _rms(z[:, o_q:o_kv], qan_ref[...])
    qf = _dot(ql.astype(BF16), wqb_ref[...])
    if rope:
        q_partner = qf[:, MLA_HEADS * MLA_HP:]
    q_ref[...] = _mla_q(qf, q_partner, gq_ref[...], g_partner, rope_tabs).astype(BF16)
    ckv = _rms(z[:, o_kv:o_pe], kvan_ref[...])
    kpe_slot = z[:, o_pe:o_pe + MLA_HP]
    k, v = _kv_up(ckv, kpe_slot, wkvb_ref, gk_ref[...], rope_tabs)
    k_ref[...] = k.astype(BF16)
    v_ref[...] = v.astype(BF16)
    if context:
        ckv_ref[...] = ckv
        kpe_ref[...] = kpe_slot
        _attend_all(q_ref, [k_ref, v_ref], att_ref, seq, MLA_HP, MLA_HEADS // 2)


def _cache_kv_kernel(ckv_ref, kpe_ref, wkvb_ref, gk_ref, k_ref, v_ref):
    k, v = _kv_up(ckv_ref[...], kpe_ref[...], wkvb_ref, gk_ref[...], None)
    k_ref[...] = k.astype(BF16)
    v_ref[...] = v.astype(BF16)


def _attend(q, sources):
    scores = []
    for k, _, bias in sources:
        s = _dot_nt(q, k)
        scores.append(s if bias is None else s + bias)
    m = scores[0].max(axis=-1, keepdims=True)
    for s in scores[1:]:
        m = jnp.maximum(m, s.max(axis=-1, keepdims=True))
    den = None
    out = None
    for s, (_, v, _) in zip(scores, sources):
        p = jnp.exp(s - m)
        ps = p.sum(axis=-1, keepdims=True)
        pv = _dot(p.astype(BF16), v)
        den = ps if den is None else den + ps
        out = pv if out is None else out + pv
    return out / den


def _pair_outputs(q_ref, q_rows, pair, dk, srcs, n_rows):
    low = lax.broadcasted_iota(jnp.int32, (n_rows, LANES), 1) < V_DIM
    vcols = slice(pair * LANES, (pair + 1) * LANES)
    outs = []
    for hh in range(2):
        if dk == NA_DH:
            q128 = q_ref[q_rows, pair * LANES:(pair + 1) * LANES]
            keep = low if hh == 0 else jnp.logical_not(low)
            q = jnp.where(keep, q128, jnp.zeros_like(q128))
            kcols = slice(pair * LANES, (pair + 1) * LANES)
        else:
            kcols = slice((2 * pair + hh) * dk, (2 * pair + hh + 1) * dk)
            q = q_ref[q_rows, kcols]
        sources = []
        for k_ref, k_rows, v_ref, v_rows, bias_pair in srcs:
            k = k_ref[k_rows, kcols].astype(BF16)
            v = v_ref[v_rows, vcols].astype(BF16)
            sources.append((k, v, None if bias_pair is None else bias_pair[hh]))
        outs.append(_attend(q, sources))
    return jnp.where(low, outs[0], outs[1])


def _attend_all(q_ref, kv, o_ref, lq, dk, n_pairs, q_chunk=Q_CHUNK):
    full = slice(None)
    for c in range(lq // q_chunk):
        rows = slice(c * q_chunk, (c + 1) * q_chunk)
        for pair in range(n_pairs):
            srcs = [(kv[2 * i], full, kv[2 * i + 1], full, None) for i in range(len(kv) // 2)]
            o = _pair_outputs(q_ref, rows, pair, dk, srcs, q_chunk)
            o_ref[rows, pair * LANES:(pair + 1) * LANES] = o.astype(o_ref.dtype)


def _attn_kernel(*refs, lq, dk, n_pairs, n_src, q_chunk):
    _attend_all(refs[0], refs[1:1 + 2 * n_src], refs[1 + 2 * n_src], lq, dk, n_pairs, q_chunk)


def _na_kernel(q_ref, k_ref, v_ref, ck_ref, cv_ref, pb_ref, o_ref, bias_ref, *, n_pairs):
    @pl.when(pl.program_id(1) == 0)
    def _():
        low = lax.broadcasted_iota(jnp.int32, (GRID_W, LANES), 1) < GRID_W
        neg = jnp.full((GRID_W, LANES), NEG_INF, F32)
        for hh in range(2 * n_pairs):
            for r in range(GRID_R):
                win0 = (r // 8) * 4
                rs = min(max(r - KH // 2, 0), GRID_R - KH)
                delta, dr0 = rs - win0, rs - r + KH - 1
                for p in range(NA_WIN_ROWS // 2):
                    ok0 = delta <= 2 * p < delta + KH
                    ok1 = delta <= 2 * p + 1 < delta + KH
                    if not (ok0 or ok1):
                        blk = neg
                    else:
                        blk = pb_ref[hh, dr0 + 2 * p - delta + 1]
                        if not ok0:
                            blk = jnp.where(low, NEG_INF, blk)
                        if not ok1:
                            blk = jnp.where(low, blk, NEG_INF)
                    bias_ref[hh, r * GRID_W:(r + 1) * GRID_W, p * LANES:(p + 1) * LANES] = blk

    full = slice(None)
    chunk_rows = LATENT_Q_CHUNK // GRID_W
    for c in range(GRID_R // chunk_rows):
        rows = slice(c * LATENT_Q_CHUNK, (c + 1) * LATENT_Q_CHUNK)
        first, last = c * chunk_rows, (c + 1) * chunk_rows - 1
        row_lo = min(max(first - KH // 2, 0), GRID_R - KH)
        row_hi = min(max(last - KH // 2, 0), GRID_R - KH) + KH
        row_hi += (row_hi - row_lo) % 2
        win0 = (first // 8) * 4
        assert (last // 8) * 4 == win0 and row_hi - win0 <= NA_WIN_ROWS
        win = slice(row_lo * GRID_W, row_hi * GRID_W)
        strip = slice((row_lo - win0) * GRID_W, (row_hi - win0) * GRID_W)
        for pair in range(n_pairs):
            bias_pair = (bias_ref[2 * pair, rows, strip], bias_ref[2 * pair + 1, rows, strip])
            srcs = [(k_ref, win, v_ref, win, bias_pair), (ck_ref, full, cv_ref, full, None)]
            o = _pair_outputs(q_ref, rows, pair, NA_DH, srcs, LATENT_Q_CHUNK)
            o_ref[rows, pair * LANES:(pair + 1) * LANES] = o.astype(o_ref.dtype)


def _out_mlp_kernel(y_ref, b1_ref, b2_ref, mod_ref, wo_ref, g2_ref, w1_ref, w2_ref, o_ref, *, hc):
    mod = mod_ref[...]
    mix = _dot(b1_ref[...], wo_ref[0:A_W, :]) + _dot(b2_ref[...], wo_ref[A_W:2 * A_W, :])
    y1 = y_ref[...] + mod[:, 2 * D:3 * D] * mix
    h2 = _modulated(y1, g2_ref[...], mod[:, 3 * D:4 * D], mod[:, 4 * D:5 * D]).astype(BF16)
    acc = jnp.zeros(y1.shape, F32)
    for c in range(MLP_HIDDEN // hc):
        a = jnp.maximum(_dot(h2, w1_ref[:, c * hc:(c + 1) * hc]), 0.0)
        acc = acc + _dot((a * a).astype(BF16), w2_ref[c * hc:(c + 1) * hc, :])
    o_ref[...] = y1 + mod[:, 5 * D:6 * D] * acc


def _params(n_grid):
    return pltpu.CompilerParams(dimension_semantics=("arbitrary",) * n_grid,
                                vmem_limit_bytes=VMEM_LIMIT)


def _layer_spec(shape, layer):
    nd = len(shape)
    return pl.BlockSpec((None,) + tuple(shape[1:]), lambda *_: (layer,) + (0,) * (nd - 1),
                        pipeline_mode=pl.Buffered(1))


def _mod_spec(layer, row_of_batch):
    return pl.BlockSpec((None, None, 1, 6 * D), lambda b, *_: (layer, row_of_batch(b), 0, 0))


def _ada_call(c_all, ada_w, ada_b):
    tn = 1536
    return pl.pallas_call(
        _ada_kernel,
        out_shape=jax.ShapeDtypeStruct((DEPTH, MOD_ROWS, 6 * D), F32),
        grid=(DEPTH, 6 * D // tn),
        in_specs=[pl.BlockSpec((MOD_ROWS, D), lambda l, j: (0, 0)),
                  pl.BlockSpec((None, D, tn), lambda l, j: (l, 0, j)),
                  pl.BlockSpec((None, 1, tn), lambda l, j: (l, 0, j))],
        out_specs=pl.BlockSpec((None, MOD_ROWS, tn), lambda l, j: (l, 0, j)),
        compiler_params=_params(2), name="ada_mod",
    )(c_all, ada_w, ada_b)


def _even_in_call(y, mods, layer, i, row_of_batch, pp, context):
    nb, seq, _ = y.shape
    tok = lambda w: pl.BlockSpec((None, seq, w), lambda b: (b, 0, 0))
    tok_t = pl.BlockSpec((None, A_W, seq), lambda b: (b, 0, 0))
    state = jax.ShapeDtypeStruct((nb, A_W, seq), F32)
    act = jax.ShapeDtypeStruct((nb, seq, A_W), BF16)
    return pl.pallas_call(
        functools.partial(_even_in_kernel, seq=seq, context=context),
        out_shape=(act, act, state, state) if context else (act,) * 4,
        grid=(nb,),
        scratch_shapes=[pltpu.VMEM((seq, A_W), BF16)] * 3 if context else [],
        in_specs=[tok(D), _mod_spec(layer, row_of_batch), _layer_spec(pp["norm1_g"].shape, layer),
                  _layer_spec(pp["even_w_in"].shape, i), _layer_spec(pp["even_conv_w"].shape, i),
                  _layer_spec(pp["na_q_norm"].shape, i), _layer_spec(pp["na_k_norm"].shape, i)],
        out_specs=(tok(A_W), tok(A_W), tok_t, tok_t) if context else (tok(A_W),) * 4,
        compiler_params=_params(1), name="even_in",
    )(y, mods, pp["norm1_g"], pp["even_w_in"], pp["even_conv_w"], pp["na_q_norm"], pp["na_k_norm"])


def _odd_in_call(y, mods, layer, i, row_of_batch, pp, rope_tabs, context):
    nb, seq, _ = y.shape
    tok = lambda w: pl.BlockSpec((None, seq, w), lambda b: (b, 0, 0))
    rope = rope_tabs is not None
    names = ["norm1_g", "odd_w_in", "pool_w", "pool_scale", "q_a_norm",
             "w_q_b_rope" if rope else "w_q_b", "kv_a_norm", "w_kv_b", "mla_q_norm", "mla_k_norm"]
    layers = [layer] + [i] * 9
    in_specs = [tok(D), _mod_spec(layer, row_of_batch)]
    in_specs += [_layer_spec(pp[n].shape, li) for n, li in zip(names, layers)]
    args = [y, mods] + [pp[n] for n in names]
    if rope:
        in_specs += [pl.BlockSpec((seq, MLA_HP), lambda b: (0, 0))] * 2
        in_specs += [_layer_spec(pp["mla_q_norm_partner"].shape, i)]
        args += list(rope_tabs) + [pp["mla_q_norm_partner"]]
    qk_w, v_w = MLA_HEADS * MLA_HP, MLA_HEADS * V_DIM
    if context:
        widths, dtypes = [A_W, v_w, KV_LORA, MLA_HP], [BF16, BF16, F32, F32]
        scratch = [pltpu.VMEM((seq, qk_w), BF16), pltpu.VMEM((seq, qk_w), BF16),
                   pltpu.VMEM((seq, v_w), BF16)]
    else:
        widths, dtypes = [A_W, qk_w, qk_w, v_w], [BF16] * 4
        scratch = []
    return pl.pallas_call(
        functools.partial(_odd_in_kernel, seq=seq, rope=rope, context=context),
        out_shape=tuple(jax.ShapeDtypeStruct((nb, seq, w), dt) for w, dt in zip(widths, dtypes)),
        grid=(nb,), in_specs=in_specs, out_specs=tuple(tok(w) for w in widths),
        scratch_shapes=scratch, compiler_params=_params(1), name="odd_in",
    )(*args)


def _cache_kv_call(cache_ckv, cache_kpe_slot, i, pp):
    nb, _, past, _ = cache_ckv.shape
    cache = lambda w: pl.BlockSpec((None, None, past, w), lambda b: (b, i, 0, 0))
    tok = lambda w: pl.BlockSpec((None, past, w), lambda b: (b, 0, 0))
    return pl.pallas_call(
        _cache_kv_kernel,
        out_shape=(jax.ShapeDtypeStruct((nb, past, MLA_HEADS * MLA_HP), BF16),
                   jax.ShapeDtypeStruct((nb, past, MLA_HEADS * V_DIM), BF16)),
        grid=(nb,),
        in_specs=[cache(KV_LORA), cache(MLA_HP), _layer_spec(pp["w_kv_b"].shape, i),
                  _layer_spec(pp["mla_k_norm"].shape, i)],
        out_specs=(tok(MLA_HEADS * MLA_HP), tok(MLA_HEADS * V_DIM)),
        compiler_params=_params(1), name="cache_kv",
    )(cache_ckv, cache_kpe_slot, pp["w_kv_b"], pp["mla_k_norm"])


def _attn_call(q, kvs, dk, pairs_per_step):
    nb, lq, _ = q.shape
    n_groups = NA_HEADS // (2 * pairs_per_step)
    qk_w, v_w = 2 * dk * pairs_per_step, LANES * pairs_per_step
    in_specs = [pl.BlockSpec((None, lq, qk_w), lambda g, b: (b, 0, g))]
    args = [q]
    for k, v in kvs:
        in_specs += [pl.BlockSpec((None, k.shape[1], qk_w), lambda g, b: (b, 0, g)),
                     pl.BlockSpec((None, v.shape[1], v_w), lambda g, b: (b, 0, g))]
        args += [k, v]
    return pl.pallas_call(
        functools.partial(_attn_kernel, lq=lq, dk=dk, n_pairs=pairs_per_step, n_src=len(kvs),
                          q_chunk=LATENT_Q_CHUNK),
        out_shape=jax.ShapeDtypeStruct((nb, lq, NA_HEADS * V_DIM), BF16),
        grid=(n_groups, nb), in_specs=in_specs,
        out_specs=pl.BlockSpec((None, lq, v_w), lambda g, b: (b, 0, g)),
        compiler_params=_params(2), name="attn",
    )(*args)


def _na_call(q, k, v, cache_k, cache_v, i, pair_bias):
    nb, lq, _ = q.shape
    past = cache_k.shape[2]
    n_pairs = 2
    width = n_pairs * LANES
    tok = pl.BlockSpec((None, lq, width), lambda g, b: (b, 0, g))
    cache = pl.BlockSpec((None, None, past, width), lambda g, b: (b, i, 0, g))
    return pl.pallas_call(
        functools.partial(_na_kernel, n_pairs=n_pairs),
        out_shape=jax.ShapeDtypeStruct((nb, lq, NA_HEADS * NA_DH), BF16),
        grid=(NA_HEADS // (2 * n_pairs), nb),
        in_specs=[tok, tok, tok, cache, cache,
                  pl.BlockSpec((None, 2 * n_pairs, 2 * KH, GRID_W, LANES),
                               lambda g, b: (i, g, 0, 0, 0))],
        out_specs=tok,
        scratch_shapes=[pltpu.VMEM((2 * n_pairs, lq, NA_WIN_ROWS * GRID_W), F32)],
        compiler_params=_params(2), name="na_attn",
    )(q, k, v, cache_k, cache_v, pair_bias)


def _out_mlp_call(y, b1, b2, mods, layer, row_of_batch, w_out, w_out_layer, pp):
    nb, seq, _ = y.shape
    tm = 1024
    tok = lambda w: pl.BlockSpec((None, tm, w), lambda b, t: (b, t, 0))
    return pl.pallas_call(
        functools.partial(_out_mlp_kernel, hc=512),
        out_shape=jax.ShapeDtypeStruct(y.shape, F32),
        grid=(nb, seq // tm),
        in_specs=[tok(D), tok(A_W), tok(A_W), _mod_spec(layer, row_of_batch),
                  _layer_spec(w_out.shape, w_out_layer), _layer_spec(pp["norm2_g"].shape, layer),
                  _layer_spec(pp["mlp_w1"].shape, layer), _layer_spec(pp["mlp_w2"].shape, layer)],
        out_specs=tok(D),
        compiler_params=_params(2), name="out_mlp",
    )(y, b1, b2, mods, w_out, pp["norm2_g"], pp["mlp_w1"], pp["mlp_w2"])


def _row(v):
    return v[:, None, :].astype(F32)


def _head_slot(v):
    return _row(jnp.pad(v, ((0, 0), (0, MLA_HP - QK_DIM))))


def _swap_rotary_halves(v):
    half = ROPE // 2
    return jnp.concatenate(
        [jnp.zeros_like(v[..., :NOPE]), v[..., NOPE + half:], v[..., NOPE:NOPE + half]], axis=-1)


def _prepare(p):
    n_odd = p["odd_w_in"].shape[0]
    pp = {}
    pp["norm1_g"], pp["norm2_g"] = _row(p["norm1_g"]), _row(p["norm2_g"])
    pp["mlp_w1"], pp["mlp_w2"] = p["mlp_w1"].astype(BF16), p["mlp_w2"].astype(BF16)
    pp["even_w_in"] = p["even_w_in"].astype(BF16)
    pp["even_conv_w"] = p["even_conv_w"].astype(F32)
    pp["na_q_norm"] = _row(jnp.tile(p["na_q_norm"], (1, NA_HEADS)))
    pp["na_k_norm"] = _row(jnp.tile(p["na_k_norm"], (1, NA_HEADS)))
    pp["even_w_out"] = p["even_w_out"].astype(BF16)
    pp["odd_w_out"] = p["odd_w_out"].astype(BF16)
    w = p["odd_w_in"]
    o_pe = A_W + Q_LORA + KV_LORA
    pe = jnp.pad(w[:, :, o_pe:], ((0, 0), (0, 0), (NOPE, MLA_HP - QK_DIM)))
    pp["odd_w_in"] = jnp.concatenate([w[:, :, :o_pe], pe], axis=-1).astype(BF16)
    pp["pool_w"] = p["pool_w"].astype(BF16)
    pp["pool_scale"] = _row(p["pool_scale"])
    pp["q_a_norm"], pp["kv_a_norm"] = _row(p["q_a_norm"]), _row(p["kv_a_norm"])
    slot_pad = ((0, 0), (0, 0), (0, 0), (0, MLA_HP - QK_DIM))
    wq = p["w_q_b"].reshape(n_odd, Q_LORA, MLA_HEADS, QK_DIM)
    wq_slots = jnp.pad(wq, slot_pad).reshape(n_odd, Q_LORA, MLA_HEADS * MLA_HP)
    wq_partner = jnp.pad(_swap_rotary_halves(wq), slot_pad).reshape(n_odd, Q_LORA, MLA_HEADS * MLA_HP)
    pp["w_q_b"] = wq_slots.astype(BF16)
    pp["w_q_b_rope"] = jnp.concatenate([wq_slots, wq_partner], axis=-1).astype(BF16)
    wkv = p["w_kv_b"].reshape(n_odd, KV_LORA, MLA_HEADS, NOPE + V_DIM)
    wk = jnp.pad(wkv[..., :NOPE], ((0, 0), (0, 0), (0, 0), (0, MLA_HP - NOPE)))
    pp["w_kv_b"] = jnp.concatenate(
        [wk.reshape(n_odd, KV_LORA, MLA_HEADS * MLA_HP),
         wkv[..., NOPE:].reshape(n_odd, KV_LORA, MLA_HEADS * V_DIM)], axis=-1).astype(BF16)
    pp["mla_q_norm"], pp["mla_k_norm"] = _head_slot(p["mla_q_norm"]), _head_slot(p["mla_k_norm"])
    pp["mla_q_norm_partner"] = _head_slot(_swap_rotary_halves(p["mla_q_norm"]))
    return pp


def _rope_tables(n):
    t = jnp.arange(n)
    row = (t // GRID_W).astype(F32)
    col = (t % GRID_W).astype(F32)
    axis_dim = ROPE // 2
    inv = 1.0 / (ROPE_BASE ** (jnp.arange(0, axis_dim, 2, dtype=F32) / axis_dim))
    ang = jnp.concatenate([row[:, None] * inv, col[:, None] * inv], axis=-1)
    cos, sin = jnp.cos(ang), jnp.sin(ang)
    tail = jnp.zeros((n, MLA_HP - QK_DIM), F32)
    rc = jnp.concatenate([jnp.ones((n, NOPE), F32), cos, cos, tail], axis=-1)
    rs = jnp.concatenate([jnp.zeros((n, NOPE), F32), -sin, sin, tail], axis=-1)
    return rc, rs


def _na_pair_bias(rpb):
    c = np.arange(GRID_W)
    start = np.clip(c - KW // 2, 0, GRID_W - KW)
    kc = np.arange(GRID_W)
    valid = (kc[None, :] >= start[:, None]) & (kc[None, :] < start[:, None] + KW)
    dc = kc[None, :] - c[:, None] + KW - 1
    onehot = ((dc[None] == np.arange(2 * KW - 1)[:, None, None]) & valid[None]).astype(np.float32)
    blocks = jnp.einsum("lhij,jck->lhick", rpb.astype(F32), onehot, precision=lax.Precision.HIGHEST)
    blocks = jnp.where(valid, blocks, NEG_INF)
    blocks = jnp.pad(blocks, ((0, 0), (0, 0), (1, 1), (0, 0), (0, 0)), constant_values=NEG_INF)
    return jnp.concatenate([blocks[:, :, :-1], blocks[:, :, 1:]], axis=-1).astype(F32)


def kernel(x_prompt, x_sample, cache_na_k, cache_na_v, cache_mla_ckv, cache_mla_kpe, c, c_ctx,
           ada_w, ada_b, norm1_g, norm2_g, mlp_w1, mlp_w2,
           even_w_in, even_conv_w, na_q_norm, na_k_norm, na_rpb, even_w_out,
           odd_w_in, pool_w, pool_scale, q_a_norm, w_q_b, kv_a_norm, w_kv_b,
           mla_q_norm, mla_k_norm, odd_w_out):
    bp, lp, _ = x_prompt.shape
    bs, ls, _ = x_sample.shape
    pp = _prepare(dict(norm1_g=norm1_g, norm2_g=norm2_g, mlp_w1=mlp_w1, mlp_w2=mlp_w2,
                       even_w_in=even_w_in, even_conv_w=even_conv_w, na_q_norm=na_q_norm,
                       na_k_norm=na_k_norm, even_w_out=even_w_out, odd_w_in=odd_w_in, pool_w=pool_w,
                       pool_scale=pool_scale, q_a_norm=q_a_norm, w_q_b=w_q_b, kv_a_norm=kv_a_norm,
                       w_kv_b=w_kv_b, mla_q_norm=mla_q_norm, mla_k_norm=mla_k_norm,
                       odd_w_out=odd_w_out))
    c_all = jnp.concatenate(
        [c_ctx[None, :], c, jnp.zeros((MOD_ROWS - 1 - bs, D), F32)], axis=0)
    mods = _ada_call(c_all, ada_w, ada_b[:, None, :]).reshape(DEPTH, MOD_ROWS, 1, 6 * D)
    rope_tabs = _rope_tables(ls)
    pair_bias = _na_pair_bias(na_rpb)
    past = cache_na_k.shape[2]
    cache_k = cache_na_k.reshape(bs, -1, past, NA_HEADS * NA_DH)
    cache_v = cache_na_v.reshape(bs, -1, past, NA_HEADS * NA_DH)
    cache_kpe_slot = jnp.pad(cache_mla_kpe, ((0, 0), (0, 0), (0, 0), (NOPE, MLA_HP - QK_DIM)))
    ctx_row = lambda b: 0
    lat_row = lambda b: b + 1

    yp, ys = x_prompt, x_sample
    na_k, na_v, mla_ckv, mla_kpe = [], [], [], []
    for l in range(DEPTH):
        i = l // 2
        if l % 2 == 0:
            ap, att_p, kt, vt = _even_in_call(yp, mods, l, i, ctx_row, pp, True)
            na_k.append(kt.reshape(bp, NA_HEADS, NA_DH, lp))
            na_v.append(vt.reshape(bp, NA_HEADS, NA_DH, lp))
            a_s, qs, k_s, vs = _even_in_call(ys, mods, l, i, lat_row, pp, False)
            att_s = _na_call(qs, k_s, vs, cache_k, cache_v, i, pair_bias)
            w_out = pp["even_w_out"]
        else:
            ap, att_p, ckvp, kpep = _odd_in_call(yp, mods, l, i, ctx_row, pp, None, True)
            mla_ckv.append(ckvp)
            mla_kpe.append(kpep[:, :, NOPE:QK_DIM])
            a_s, qs, k_s, vs = _odd_in_call(ys, mods, l, i, lat_row, pp, rope_tabs, False)
            kc, vc = _cache_kv_call(cache_mla_ckv, cache_kpe_slot, i, pp)
            att_s = _attn_call(qs, [(kc, vc), (k_s, vs)], MLA_HP, 2)
            w_out = pp["odd_w_out"]
        yp = _out_mlp_call(yp.reshape(1, bp * lp, D), ap.reshape(1, bp * lp, A_W),
                           att_p.reshape(1, bp * lp, A_W), mods, l, ctx_row, w_out, i,
                           pp).reshape(bp, lp, D)
        ys = _out_mlp_call(ys, a_s, att_s, mods, l, lat_row, w_out, i, pp)
    state_k = jnp.transpose(jnp.stack(na_k, axis=0), (1, 0, 4, 2, 3))
    state_v = jnp.transpose(jnp.stack(na_v, axis=0), (1, 0, 4, 2, 3))
    return (yp, ys, state_k, state_v, jnp.stack(mla_ckv, axis=1), jnp.stack(mla_kpe, axis=1))
```

```python
import functools

import jax
import jax.numpy as jnp
import numpy as np
from jax import lax
from jax.experimental import pallas as pl
from jax.experimental.pallas import tpu as pltpu

F32 = jnp.float32
BF16 = jnp.bfloat16

D = 1024
DEPTH = 4
GRID_W = 64
GRID_R = 16
A_W = 512
NA_HEADS = 8
NA_DH = 64
KH = 8
KW = 16
POOL_WINDOWS = (2, 4, 8, 16)
POOL_G = 128
POOL_PAD = 16
MLA_HEADS = 8
NOPE = 64
ROPE = 32
QK_DIM = NOPE + ROPE
MLA_HP = 128
V_DIM = 64
Q_LORA = 384
KV_LORA = 256
ROPE_BASE = 10000.0
MLP_HIDDEN = 4 * D
NEG_INF = -1e30
EPS = 1e-6
MOD_ROWS = 16
LANES = 128
NA_WIN_ROWS = 12
Q_CHUNK = 256
CONTEXT_SEQS_PER_STEP = 4
LATENT_Q_CHUNK = 512
VMEM_LIMIT = 56 * 1024 * 1024


def _dot(a, b):
    return jnp.dot(a, b, preferred_element_type=F32)


def _dot_nt(a, b):
    return lax.dot_general(a, b, (((1,), (1,)), ((), ())), preferred_element_type=F32)


def _rms(x, g):
    ms = jnp.mean(x * x, axis=-1, keepdims=True)
    return x * lax.rsqrt(ms + EPS) * g


def _modulated(x, g, shift, scale):
    return _rms(x, g) * (1.0 + scale) + shift


def _head_rms_64(x, g):
    t_rows = x.shape[0]
    low = lax.broadcasted_iota(jnp.int32, (t_rows, LANES), 1) < NA_DH
    cols = []
    for j in range(x.shape[1] // LANES):
        t = x[:, j * LANES:(j + 1) * LANES]
        t2 = t * t
        s_lo = jnp.sum(jnp.where(low, t2, 0.0), axis=-1, keepdims=True)
        s_hi = jnp.sum(jnp.where(low, 0.0, t2), axis=-1, keepdims=True)
        r_lo = lax.rsqrt(s_lo * (1.0 / NA_DH) + EPS)
        r_hi = lax.rsqrt(s_hi * (1.0 / NA_DH) + EPS)
        cols.append(t * jnp.where(low, r_lo, r_hi))
    return jnp.concatenate(cols, axis=-1) * g


def _slot_inv_rms(t):
    return lax.rsqrt(jnp.sum(t * t, axis=-1, keepdims=True) * (1.0 / QK_DIM) + EPS)


def _rotary_partner(t):
    x1_side = lax.broadcasted_iota(jnp.int32, t.shape, 1) < NOPE + ROPE // 2
    return jnp.where(x1_side, pltpu.roll(t, LANES - ROPE // 2, 1), pltpu.roll(t, ROPE // 2, 1))


def _mla_q(qf, q_partner, g_slot, g_partner, rope_tabs):
    if rope_tabs is not None:
        rc, rs = rope_tabs
        gc, gs = g_slot * rc, g_partner * rs
    cols = []
    for j in range(MLA_HEADS):
        t = qf[:, j * MLA_HP:(j + 1) * MLA_HP]
        if rope_tabs is None:
            u = t * g_slot
        else:
            u = t * gc + q_partner[:, j * MLA_HP:(j + 1) * MLA_HP] * gs
        cols.append(u * (_slot_inv_rms(t) * (QK_DIM ** -0.5)))
    return jnp.concatenate(cols, axis=-1)


def _kv_up(ckv, kpe_slot, wkvb_ref, g_slot, rope_tabs):
    kvb = _dot(ckv.astype(BF16), wkvb_ref[...])
    kr = kpe_slot * g_slot
    if rope_tabs is not None:
        rc, rs = rope_tabs
        kr = kr * rc + _rotary_partner(kr) * rs
    cols = []
    for j in range(MLA_HEADS):
        kn = kvb[:, j * MLA_HP:(j + 1) * MLA_HP]
        cols.append((kn * g_slot + kr) * _slot_inv_rms(kn + kpe_slot))
    return jnp.concatenate(cols, axis=-1), kvb[:, MLA_HEADS * MLA_HP:]


def _ada_kernel(c_ref, w_ref, b_ref, o_ref):
    c = c_ref[...]
    s = c * (1.0 / (1.0 + jnp.exp(-c)))
    o_ref[...] = _dot(s.astype(BF16), w_ref[...].astype(BF16)) + b_ref[...]


def _even_in_kernel(y_ref, mod_ref, g1_ref, w_ref, cw_ref, gq_ref, gk_ref, *out_refs,
                    seq, nseq, context):
    if context:
        a_ref, att_ref, kt_ref, vt_ref, q_ref, k_ref, v_ref = out_refs
    else:
        a_ref, q_ref, k_ref, v_ref = out_refs
    mod = mod_ref[...]
    x = y_ref[...].reshape(nseq * seq, D)
    h = _modulated(x, g1_ref[...], mod[:, 0:D], mod[:, D:2 * D])
    z = _dot(h.astype(BF16), w_ref[...])
    q_all = (_head_rms_64(z[:, 3 * A_W:4 * A_W], gq_ref[...]) * (NA_DH ** -0.5)).astype(BF16)
    k_all = _head_rms_64(z[:, 4 * A_W:5 * A_W], gk_ref[...])
    v_all = z[:, 5 * A_W:6 * A_W]
    t = lax.broadcasted_iota(jnp.int32, (seq, A_W), 0)
    cw = cw_ref[...]
    for s in range(nseq):
        rows = slice(s * seq, (s + 1) * seq)
        u = z[rows, A_W:2 * A_W] * z[rows, 2 * A_W:3 * A_W]
        prev = jnp.where(t > 0, pltpu.roll(u, 1, 0), 0.0)
        nxt = jnp.where(t < seq - 1, pltpu.roll(u, seq - 1, 0), 0.0)
        a = z[rows, 0:A_W] * (prev * cw[0:1] + u * cw[1:2] + nxt * cw[2:3])
        a_ref[s] = a.astype(BF16)
        q_ref[s] = q_all[rows]
        k_ref[s] = k_all[rows].astype(BF16)
        v_ref[s] = v_all[rows].astype(BF16)
        if context:
            kt_ref[s] = k_all[rows].T
            vt_ref[s] = v_all[rows].T
            _attend_all(q_ref.at[s], [k_ref.at[s], v_ref.at[s]], att_ref.at[s], seq, NA_DH,
                        NA_HEADS // 2)


def _pool_branch(u, pw_ref, ps, seq):
    zpad = jnp.zeros((POOL_PAD, POOL_G), F32)
    padded = seq + 2 * POOL_PAD
    t = lax.broadcasted_iota(jnp.int32, (seq, POOL_G), 0)
    cols = []
    for g, w in enumerate(POOL_WINDOWS):
        ug = u[:, g * POOL_G:(g + 1) * POOL_G]
        xp = jnp.concatenate([zpad, ug, zpad], axis=0)
        acc = xp + pltpu.roll(xp, 1, 0)
        s = 1
        while 4 * s <= w:
            acc = pltpu.roll(acc, s, 0) + pltpu.roll(acc, padded - s, 0)
            s *= 2
        win = acc[POOL_PAD:POOL_PAD + seq]
        lo = jnp.maximum(t - w // 2, 0)
        hi = jnp.minimum(t - w // 2 + w, seq)
        pooled = win / (hi - lo).astype(F32)
        cols.append(_dot((pooled - ug).astype(BF16), pw_ref[g]))
    return jnp.concatenate(cols, axis=-1) * ps


def _odd_in_kernel(*refs, seq, nseq, rope, context):
    (y_ref, mod_ref, g1_ref, w_ref, pw_ref, ps_ref, qan_ref, wqb_ref, kvan_ref, wkvb_ref,
     gq_ref, gk_ref) = refs[:12]
    pos = 12
    rope_tabs = g_partner = q_partner = None
    if rope:
        rope_tabs = (refs[pos][...], refs[pos + 1][...])
        g_partner = refs[pos + 2][...]
        pos += 3
    if context:
        p_ref, att_ref, ckv_ref, kpe_ref, q_ref, k_ref, v_ref = refs[pos:pos + 7]
    else:
        p_ref, q_ref, k_ref, v_ref = refs[pos:pos + 4]
    assert not (rope and nseq > 1)
    mod = mod_ref[...]
    x = y_ref[...].reshape(nseq * seq, D)
    h = _modulated(x, g1_ref[...], mod[:, 0:D], mod[:, D:2 * D])
    z = _dot(h.astype(BF16), w_ref[...])
    o_q, o_kv, o_pe = A_W, A_W + Q_LORA, A_W + Q_LORA + KV_LORA
    ql = _rms(z[:, o_q:o_kv], qan_ref[...])
    qf = _dot(ql.astype(BF16), wqb_ref[...])
    if rope:
        q_partner = qf[:, MLA_HEADS * MLA_HP:]
    q_all = _mla_q(qf, q_partner, gq_ref[...], g_partner, rope_tabs).astype(BF16)
    ckv = _rms(z[:, o_kv:o_pe], kvan_ref[...])
    kpe_slot = z[:, o_pe:o_pe + MLA_HP]
    k_all, v_all = _kv_up(ckv, kpe_slot, wkvb_ref, gk_ref[...], rope_tabs)
    for s in range(nseq):
        rows = slice(s * seq, (s + 1) * seq)
        p_ref[s] = _pool_branch(z[rows, 0:A_W], pw_ref, ps_ref[...], seq).astype(BF16)
        q_ref[s] = q_all[rows]
        k_ref[s] = k_all[rows].astype(BF16)
        v_ref[s] = v_all[rows].astype(BF16)
        if context:
            ckv_ref[s] = ckv[rows]
            kpe_ref[s] = kpe_slot[rows]
            _attend_all(q_ref.at[s], [k_ref.at[s], v_ref.at[s]], att_ref.at[s], seq, MLA_HP,
                        MLA_HEADS // 2)


def _cache_kv_kernel(ckv_ref, kpe_ref, wkvb_ref, gk_ref, k_ref, v_ref):
    k, v = _kv_up(ckv_ref[...], kpe_ref[...], wkvb_ref, gk_ref[...], None)
    k_ref[...] = k.astype(BF16)
    v_ref[...] = v.astype(BF16)


def _attend(q, sources):
    scores = []
    for k, _, bias in sources:
        s = _dot_nt(q, k)
        scores.append(s if bias is None else s + bias)
    m = scores[0].max(axis=-1, keepdims=True)
    for s in scores[1:]:
        m = jnp.maximum(m, s.max(axis=-1, keepdims=True))
    den = None
    out = None
    for s, (_, v, _) in zip(scores, sources):
        p = jnp.exp(s - m)
        ps = p.sum(axis=-1, keepdims=True)
        pv = _dot(p.astype(BF16), v)
        den = ps if den is None else den + ps
        out = pv if out is None else out + pv
    return out / den


def _pair_outputs(q_ref, q_rows, pair, dk, srcs, n_rows):
    low = lax.broadcasted_iota(jnp.int32, (n_rows, LANES), 1) < V_DIM
    vcols = slice(pair * LANES, (pair + 1) * LANES)
    outs = []
    for hh in range(2):
        if dk == NA_DH:
            q128 = q_ref[q_rows, pair * LANES:(pair + 1) * LANES]
            keep = low if hh == 0 else jnp.logical_not(low)
            q = jnp.where(keep, q128, jnp.zeros_like(q128))
            kcols = slice(pair * LANES, (pair + 1) * LANES)
        else:
            kcols = slice((2 * pair + hh) * dk, (2 * pair + hh + 1) * dk)
            q = q_ref[q_rows, kcols]
        sources = []
        for k_ref, k_rows, v_ref, v_rows, bias_pair in srcs:
            k = k_ref[k_rows, kcols].astype(BF16)
            v = v_ref[v_rows, vcols].astype(BF16)
            sources.append((k, v, None if bias_pair is None else bias_pair[hh]))
        outs.append(_attend(q, sources))
    return jnp.where(low, outs[0], outs[1])


def _attend_all(q_ref, kv, o_ref, lq, dk, n_pairs, q_chunk=Q_CHUNK):
    full = slice(None)
    for c in range(lq // q_chunk):
        rows = slice(c * q_chunk, (c + 1) * q_chunk)
        for pair in range(n_pairs):
            srcs = [(kv[2 * i], full, kv[2 * i + 1], full, None) for i in range(len(kv) // 2)]
            o = _pair_outputs(q_ref, rows, pair, dk, srcs, q_chunk)
            o_ref[rows, pair * LANES:(pair + 1) * LANES] = o.astype(o_ref.dtype)


def _attn_kernel(*refs, lq, dk, n_pairs, n_src, q_chunk):
    _attend_all(refs[0], refs[1:1 + 2 * n_src], refs[1 + 2 * n_src], lq, dk, n_pairs, q_chunk)


def _mla_attn_kernel(q_ref, ckv_ref, kpe_ref, wk_ref, wv_ref, gk_ref, k_ref, v_ref, o_ref,
                     kc_ref, vc_ref, *, lq, n_pairs):
    ckv = ckv_ref[...].astype(BF16)
    kn_all = _dot(ckv, wk_ref[...])
    kpe_slot, g_slot = kpe_ref[...], gk_ref[...]
    kr = kpe_slot * g_slot
    for j in range(2 * n_pairs):
        kn = kn_all[:, j * MLA_HP:(j + 1) * MLA_HP]
        kc = (kn * g_slot + kr) * _slot_inv_rms(kn + kpe_slot)
        kc_ref[:, j * MLA_HP:(j + 1) * MLA_HP] = kc.astype(BF16)
    vc_ref[...] = _dot(ckv, wv_ref[...]).astype(BF16)
    _attend_all(q_ref, [kc_ref, vc_ref, k_ref, v_ref], o_ref, lq, MLA_HP, n_pairs, LATENT_Q_CHUNK)


def _na_kernel(q_ref, k_ref, v_ref, ck_ref, cv_ref, pb_ref, o_ref, bias_ref, *, n_pairs):
    @pl.when(pl.program_id(1) == 0)
    def _():
        low = lax.broadcasted_iota(jnp.int32, (GRID_W, LANES), 1) < GRID_W
        neg = jnp.full((GRID_W, LANES), NEG_INF, F32)
        for hh in range(2 * n_pairs):
            for r in range(GRID_R):
                win0 = (r // 8) * 4
                rs = min(max(r - KH // 2, 0), GRID_R - KH)
                delta, dr0 = rs - win0, rs - r + KH - 1
                for p in range(NA_WIN_ROWS // 2):
                    ok0 = delta <= 2 * p < delta + KH
                    ok1 = delta <= 2 * p + 1 < delta + KH
                    if not (ok0 or ok1):
                        blk = neg
                    else:
                        blk = pb_ref[hh, dr0 + 2 * p - delta + 1]
                        if not ok0:
                            blk = jnp.where(low, NEG_INF, blk)
                        if not ok1:
                            blk = jnp.where(low, blk, NEG_INF)
                    bias_ref[hh, r * GRID_W:(r + 1) * GRID_W, p * LANES:(p + 1) * LANES] = blk

    full = slice(None)
    chunk_rows = LATENT_Q_CHUNK // GRID_W
    for c in range(GRID_R // chunk_rows):
        rows = slice(c * LATENT_Q_CHUNK, (c + 1) * LATENT_Q_CHUNK)
        first, last = c * chunk_rows, (c + 1) * chunk_rows - 1
        row_lo = min(max(first - KH // 2, 0), GRID_R - KH)
        row_hi = min(max(last - KH // 2, 0), GRID_R - KH) + KH
        row_hi += (row_hi - row_lo) % 2
        win0 = (first // 8) * 4
        assert (last // 8) * 4 == win0 and row_hi - win0 <= NA_WIN_ROWS
        win = slice(row_lo * GRID_W, row_hi * GRID_W)
        strip = slice((row_lo - win0) * GRID_W, (row_hi - win0) * GRID_W)
        for pair in range(n_pairs):
            bias_pair = (bias_ref[2 * pair, rows, strip], bias_ref[2 * pair + 1, rows, strip])
            srcs = [(k_ref, win, v_ref, win, bias_pair), (ck_ref, full, cv_ref, full, None)]
            o = _pair_outputs(q_ref, rows, pair, NA_DH, srcs, LATENT_Q_CHUNK)
            o_ref[rows, pair * LANES:(pair + 1) * LANES] = o.astype(o_ref.dtype)


def _out_mlp_kernel(y_ref, b1_ref, b2_ref, mod_ref, wo_ref, g2_ref, w1_ref, w2_ref, o_ref, *, hc):
    mod = mod_ref[...]
    mix = _dot(b1_ref[...], wo_ref[0:A_W, :]) + _dot(b2_ref[...], wo_ref[A_W:2 * A_W, :])
    y1 = y_ref[...] + mod[:, 2 * D:3 * D] * mix
    h2 = _modulated(y1, g2_ref[...], mod[:, 3 * D:4 * D], mod[:, 4 * D:5 * D]).astype(BF16)
    acc = jnp.zeros(y1.shape, F32)
    for c in range(MLP_HIDDEN // hc):
        a = jnp.maximum(_dot(h2, w1_ref[:, c * hc:(c + 1) * hc]), 0.0)
        acc = acc + _dot((a * a).astype(BF16), w2_ref[c * hc:(c + 1) * hc, :])
    o_ref[...] = y1 + mod[:, 5 * D:6 * D] * acc


def _params(n_grid):
    return pltpu.CompilerParams(dimension_semantics=("arbitrary",) * n_grid,
                                vmem_limit_bytes=VMEM_LIMIT)


def _layer_spec(shape, layer):
    nd = len(shape)
    return pl.BlockSpec((None,) + tuple(shape[1:]), lambda *_: (layer,) + (0,) * (nd - 1),
                        pipeline_mode=pl.Buffered(1))


def _mod_spec(layer, row_of_batch):
    return pl.BlockSpec((None, None, 1, 6 * D), lambda b, *_: (layer, row_of_batch(b), 0, 0))


def _ada_call(c_all, ada_w, ada_b):
    tn = 1536
    return pl.pallas_call(
        _ada_kernel,
        out_shape=jax.ShapeDtypeStruct((DEPTH, MOD_ROWS, 6 * D), F32),
        grid=(DEPTH, 6 * D // tn),
        in_specs=[pl.BlockSpec((MOD_ROWS, D), lambda l, j: (0, 0)),
                  pl.BlockSpec((None, D, tn), lambda l, j: (l, 0, j)),
                  pl.BlockSpec((None, 1, tn), lambda l, j: (l, 0, j))],
        out_specs=pl.BlockSpec((None, MOD_ROWS, tn), lambda l, j: (l, 0, j)),
        compiler_params=_params(2), name="ada_mod",
    )(c_all, ada_w, ada_b)


def _even_in_call(y, mods, layer, i, row_of_batch, pp, context):
    nb, seq, _ = y.shape
    nseq = CONTEXT_SEQS_PER_STEP if context else 1
    tok = lambda w: pl.BlockSpec((nseq, seq, w), lambda b: (b, 0, 0))
    tok_t = pl.BlockSpec((nseq, A_W, seq), lambda b: (b, 0, 0))
    state = jax.ShapeDtypeStruct((nb, A_W, seq), F32)
    act = jax.ShapeDtypeStruct((nb, seq, A_W), BF16)
    return pl.pallas_call(
        functools.partial(_even_in_kernel, seq=seq, nseq=nseq, context=context),
        out_shape=(act, act, state, state) if context else (act,) * 4,
        grid=(nb // nseq,),
        scratch_shapes=[pltpu.VMEM((nseq, seq, A_W), BF16)] * 3 if context else [],
        in_specs=[tok(D), _mod_spec(layer, row_of_batch), _layer_spec(pp["norm1_g"].shape, layer),
                  _layer_spec(pp["even_w_in"].shape, i), _layer_spec(pp["even_conv_w"].shape, i),
                  _layer_spec(pp["na_q_norm"].shape, i), _layer_spec(pp["na_k_norm"].shape, i)],
        out_specs=(tok(A_W), tok(A_W), tok_t, tok_t) if context else (tok(A_W),) * 4,
        compiler_params=_params(1), name="even_in",
    )(y, mods, pp["norm1_g"], pp["even_w_in"], pp["even_conv_w"], pp["na_q_norm"], pp["na_k_norm"])


def _odd_in_call(y, mods, layer, i, row_of_batch, pp, rope_tabs, context):
    nb, seq, _ = y.shape
    nseq = CONTEXT_SEQS_PER_STEP if context else 1
    tok = lambda w: pl.BlockSpec((nseq, seq, w), lambda b: (b, 0, 0))
    rope = rope_tabs is not None
    names = ["norm1_g", "odd_w_in", "pool_w", "pool_scale", "q_a_norm",
             "w_q_b_rope" if rope else "w_q_b", "kv_a_norm", "w_kv_b", "mla_q_norm", "mla_k_norm"]
    layers = [layer] + [i] * 9
    in_specs = [tok(D), _mod_spec(layer, row_of_batch)]
    in_specs += [_layer_spec(pp[n].shape, li) for n, li in zip(names, layers)]
    args = [y, mods] + [pp[n] for n in names]
    if rope:
        in_specs += [pl.BlockSpec((seq, MLA_HP), lambda b: (0, 0))] * 2
        in_specs += [_layer_spec(pp["mla_q_norm_partner"].shape, i)]
        args += list(rope_tabs) + [pp["mla_q_norm_partner"]]
    qk_w, v_w = MLA_HEADS * MLA_HP, MLA_HEADS * V_DIM
    if context:
        widths, dtypes = [A_W, v_w, KV_LORA, MLA_HP], [BF16, BF16, F32, F32]
        scratch = [pltpu.VMEM((nseq, seq, qk_w), BF16), pltpu.VMEM((nseq, seq, qk_w), BF16),
                   pltpu.VMEM((nseq, seq, v_w), BF16)]
    else:
        widths, dtypes = [A_W, qk_w, qk_w, v_w], [BF16] * 4
        scratch = []
    return pl.pallas_call(
        functools.partial(_odd_in_kernel, seq=seq, nseq=nseq, rope=rope, context=context),
        out_shape=tuple(jax.ShapeDtypeStruct((nb, seq, w), dt) for w, dt in zip(widths, dtypes)),
        grid=(nb // nseq,), in_specs=in_specs, out_specs=tuple(tok(w) for w in widths),
        scratch_shapes=scratch, compiler_params=_params(1), name="odd_in",
    )(*args)


def _cache_kv_call(cache_ckv, cache_kpe_slot, i, pp):
    nb, _, past, _ = cache_ckv.shape
    cache = lambda w: pl.BlockSpec((None, None, past, w), lambda b: (b, i, 0, 0))
    tok = lambda w: pl.BlockSpec((None, past, w), lambda b: (b, 0, 0))
    return pl.pallas_call(
        _cache_kv_kernel,
        out_shape=(jax.ShapeDtypeStruct((nb, past, MLA_HEADS * MLA_HP), BF16),
                   jax.ShapeDtypeStruct((nb, past, MLA_HEADS * V_DIM), BF16)),
        grid=(nb,),
        in_specs=[cache(KV_LORA), cache(MLA_HP), _layer_spec(pp["w_kv_b"].shape, i),
                  _layer_spec(pp["mla_k_norm"].shape, i)],
        out_specs=(tok(MLA_HEADS * MLA_HP), tok(MLA_HEADS * V_DIM)),
        compiler_params=_params(1), name="cache_kv",
    )(cache_ckv, cache_kpe_slot, pp["w_kv_b"], pp["mla_k_norm"])


def _attn_call(q, kvs, dk, pairs_per_step):
    nb, lq, _ = q.shape
    n_groups = NA_HEADS // (2 * pairs_per_step)
    qk_w, v_w = 2 * dk * pairs_per_step, LANES * pairs_per_step
    in_specs = [pl.BlockSpec((None, lq, qk_w), lambda g, b: (b, 0, g))]
    args = [q]
    for k, v in kvs:
        in_specs += [pl.BlockSpec((None, k.shape[1], qk_w), lambda g, b: (b, 0, g)),
                     pl.BlockSpec((None, v.shape[1], v_w), lambda g, b: (b, 0, g))]
        args += [k, v]
    return pl.pallas_call(
        functools.partial(_attn_kernel, lq=lq, dk=dk, n_pairs=pairs_per_step, n_src=len(kvs),
                          q_chunk=LATENT_Q_CHUNK),
        out_shape=jax.ShapeDtypeStruct((nb, lq, NA_HEADS * V_DIM), BF16),
        grid=(n_groups, nb), in_specs=in_specs,
        out_specs=pl.BlockSpec((None, lq, v_w), lambda g, b: (b, 0, g)),
        compiler_params=_params(2), name="attn",
    )(*args)


def _mla_attn_call(q, k, v, cache_ckv, cache_kpe_slot, i, pp):
    nb, lq, _ = q.shape
    past = cache_ckv.shape[2]
    n_pairs = 2
    n_groups = MLA_HEADS // (2 * n_pairs)
    qk_w, v_w = 2 * MLA_HP * n_pairs, LANES * n_pairs
    k_blocks = MLA_HEADS * MLA_HP // v_w
    cache = lambda w: pl.BlockSpec((None, None, past, w), lambda g, b: (b, i, 0, 0))
    return pl.pallas_call(
        functools.partial(_mla_attn_kernel, lq=lq, n_pairs=n_pairs),
        out_shape=jax.ShapeDtypeStruct((nb, lq, MLA_HEADS * V_DIM), BF16),
        grid=(n_groups, nb),
        in_specs=[pl.BlockSpec((None, lq, qk_w), lambda g, b: (b, 0, g)),
                  cache(KV_LORA), cache(MLA_HP),
                  pl.BlockSpec((None, KV_LORA, qk_w), lambda g, b: (i, 0, g)),
                  pl.BlockSpec((None, KV_LORA, v_w), lambda g, b: (i, 0, k_blocks + g)),
                  _layer_spec(pp["mla_k_norm"].shape, i),
                  pl.BlockSpec((None, lq, qk_w), lambda g, b: (b, 0, g)),
                  pl.BlockSpec((None, lq, v_w), lambda g, b: (b, 0, g))],
        out_specs=pl.BlockSpec((None, lq, v_w), lambda g, b: (b, 0, g)),
        scratch_shapes=[pltpu.VMEM((past, qk_w), BF16), pltpu.VMEM((past, v_w), BF16)],
        compiler_params=_params(2), name="mla_attn",
    )(q, cache_ckv, cache_kpe_slot, pp["w_kv_b"], pp["w_kv_b"], pp["mla_k_norm"], k, v)


def _na_call(q, k, v, cache_k, cache_v, i, pair_bias):
    nb, lq, _ = q.shape
    past = cache_k.shape[2]
    n_pairs = 2
    width = n_pairs * LANES
    tok = pl.BlockSpec((None, lq, width), lambda g, b: (b, 0, g))
    cache = pl.BlockSpec((None, None, past, width), lambda g, b: (b, i, 0, g))
    return pl.pallas_call(
        functools.partial(_na_kernel, n_pairs=n_pairs),
        out_shape=jax.ShapeDtypeStruct((nb, lq, NA_HEADS * NA_DH), BF16),
        grid=(NA_HEADS // (2 * n_pairs), nb),
        in_specs=[tok, tok, tok, cache, cache,
                  pl.BlockSpec((None, 2 * n_pairs, 2 * KH, GRID_W, LANES),
                               lambda g, b: (i, g, 0, 0, 0))],
        out_specs=tok,
        scratch_shapes=[pltpu.VMEM((2 * n_pairs, lq, NA_WIN_ROWS * GRID_W), F32)],
        compiler_params=_params(2), name="na_attn",
    )(q, k, v, cache_k, cache_v, pair_bias)


def _out_mlp_call(y, b1, b2, mods, layer, row_of_batch, w_out, w_out_layer, pp):
    nb, seq, _ = y.shape
    tm = 1024
    tok = lambda w: pl.BlockSpec((None, tm, w), lambda b, t: (b, t, 0))
    return pl.pallas_call(
        functools.partial(_out_mlp_kernel, hc=512),
        out_shape=jax.ShapeDtypeStruct(y.shape, F32),
        grid=(nb, seq // tm),
        in_specs=[tok(D), tok(A_W), tok(A_W), _mod_spec(layer, row_of_batch),
                  _layer_spec(w_out.shape, w_out_layer), _layer_spec(pp["norm2_g"].shape, layer),
                  _layer_spec(pp["mlp_w1"].shape, layer), _layer_spec(pp["mlp_w2"].shape, layer)],
        out_specs=tok(D),
        compiler_params=_params(2), name="out_mlp",
    )(y, b1, b2, mods, w_out, pp["norm2_g"], pp["mlp_w1"], pp["mlp_w2"])


def _row(v):
    return v[:, None, :].astype(F32)


def _head_slot(v):
    return _row(jnp.pad(v, ((0, 0), (0, MLA_HP - QK_DIM))))


def _swap_rotary_halves(v):
    half = ROPE // 2
    return jnp.concatenate(
        [jnp.zeros_like(v[..., :NOPE]), v[..., NOPE + half:], v[..., NOPE:NOPE + half]], axis=-1)


def _prepare(p):
    n_odd = p["odd_w_in"].shape[0]
    pp = {}
    pp["norm1_g"], pp["norm2_g"] = _row(p["norm1_g"]), _row(p["norm2_g"])
    pp["mlp_w1"], pp["mlp_w2"] = p["mlp_w1"].astype(BF16), p["mlp_w2"].astype(BF16)
    pp["even_w_in"] = p["even_w_in"].astype(BF16)
    pp["even_conv_w"] = p["even_conv_w"].astype(F32)
    pp["na_q_norm"] = _row(jnp.tile(p["na_q_norm"], (1, NA_HEADS)))
    pp["na_k_norm"] = _row(jnp.tile(p["na_k_norm"], (1, NA_HEADS)))
    pp["even_w_out"] = p["even_w_out"].astype(BF16)
    pp["odd_w_out"] = p["odd_w_out"].astype(BF16)
    w = p["odd_w_in"]
    o_pe = A_W + Q_LORA + KV_LORA
    pe = jnp.pad(w[:, :, o_pe:], ((0, 0), (0, 0), (NOPE, MLA_HP - QK_DIM)))
    pp["odd_w_in"] = jnp.concatenate([w[:, :, :o_pe], pe], axis=-1).astype(BF16)
    pp["pool_w"] = p["pool_w"].astype(BF16)
    pp["pool_scale"] = _row(p["pool_scale"])
    pp["q_a_norm"], pp["kv_a_norm"] = _row(p["q_a_norm"]), _row(p["kv_a_norm"])
    slot_pad = ((0, 0), (0, 0), (0, 0), (0, MLA_HP - QK_DIM))
    wq = p["w_q_b"].reshape(n_odd, Q_LORA, MLA_HEADS, QK_DIM)
    wq_slots = jnp.pad(wq, slot_pad).reshape(n_odd, Q_LORA, MLA_HEADS * MLA_HP)
    wq_partner = jnp.pad(_swap_rotary_halves(wq), slot_pad).reshape(n_odd, Q_LORA, MLA_HEADS * MLA_HP)
    pp["w_q_b"] = wq_slots.astype(BF16)
    pp["w_q_b_rope"] = jnp.concatenate([wq_slots, wq_partner], axis=-1).astype(BF16)
    wkv = p["w_kv_b"].reshape(n_odd, KV_LORA, MLA_HEADS, NOPE + V_DIM)
    wk = jnp.pad(wkv[..., :NOPE], ((0, 0), (0, 0), (0, 0), (0, MLA_HP - NOPE)))
    pp["w_kv_b"] = jnp.concatenate(
        [wk.reshape(n_odd, KV_LORA, MLA_HEADS * MLA_HP),
         wkv[..., NOPE:].reshape(n_odd, KV_LORA, MLA_HEADS * V_DIM)], axis=-1).astype(BF16)
    pp["mla_q_norm"], pp["mla_k_norm"] = _head_slot(p["mla_q_norm"]), _head_slot(p["mla_k_norm"])
    pp["mla_q_norm_partner"] = _head_slot(_swap_rotary_halves(p["mla_q_norm"]))
    return pp


def _rope_tables(n):
    t = jnp.arange(n)
    row = (t // GRID_W).astype(F32)
    col = (t % GRID_W).astype(F32)
    axis_dim = ROPE // 2
    inv = 1.0 / (ROPE_BASE ** (jnp.arange(0, axis_dim, 2, dtype=F32) / axis_dim))
    ang = jnp.concatenate([row[:, None] * inv, col[:, None] * inv], axis=-1)
    cos, sin = jnp.cos(ang), jnp.sin(ang)
    tail = jnp.zeros((n, MLA_HP - QK_DIM), F32)
    rc = jnp.concatenate([jnp.ones((n, NOPE), F32), cos, cos, tail], axis=-1)
    rs = jnp.concatenate([jnp.zeros((n, NOPE), F32), -sin, sin, tail], axis=-1)
    return rc, rs


def _na_pair_bias(rpb):
    c = np.arange(GRID_W)
    start = np.clip(c - KW // 2, 0, GRID_W - KW)
    kc = np.arange(GRID_W)
    valid = (kc[None, :] >= start[:, None]) & (kc[None, :] < start[:, None] + KW)
    dc = kc[None, :] - c[:, None] + KW - 1
    onehot = ((dc[None] == np.arange(2 * KW - 1)[:, None, None]) & valid[None]).astype(np.float32)
    blocks = jnp.einsum("lhij,jck->lhick", rpb.astype(F32), onehot, precision=lax.Precision.HIGHEST)
    blocks = jnp.where(valid, blocks, NEG_INF)
    blocks = jnp.pad(blocks, ((0, 0), (0, 0), (1, 1), (0, 0), (0, 0)), constant_values=NEG_INF)
    return jnp.concatenate([blocks[:, :, :-1], blocks[:, :, 1:]], axis=-1).astype(F32)


def kernel(x_prompt, x_sample, cache_na_k, cache_na_v, cache_mla_ckv, cache_mla_kpe, c, c_ctx,
           ada_w, ada_b, norm1_g, norm2_g, mlp_w1, mlp_w2,
           even_w_in, even_conv_w, na_q_norm, na_k_norm, na_rpb, even_w_out,
           odd_w_in, pool_w, pool_scale, q_a_norm, w_q_b, kv_a_norm, w_kv_b,
           mla_q_norm, mla_k_norm, odd_w_out):
    bp, lp, _ = x_prompt.shape
    bs, ls, _ = x_sample.shape
    pp = _prepare(dict(norm1_g=norm1_g, norm2_g=norm2_g, mlp_w1=mlp_w1, mlp_w2=mlp_w2,
                       even_w_in=even_w_in, even_conv_w=even_conv_w, na_q_norm=na_q_norm,
                       na_k_norm=na_k_norm, even_w_out=even_w_out, odd_w_in=odd_w_in, pool_w=pool_w,
                       pool_scale=pool_scale, q_a_norm=q_a_norm, w_q_b=w_q_b, kv_a_norm=kv_a_norm,
                       w_kv_b=w_kv_b, mla_q_norm=mla_q_norm, mla_k_norm=mla_k_norm,
                       odd_w_out=odd_w_out))
    c_all = jnp.concatenate(
        [c_ctx[None, :], c, jnp.zeros((MOD_ROWS - 1 - bs, D), F32)], axis=0)
    mods = _ada_call(c_all, ada_w, ada_b[:, None, :]).reshape(DEPTH, MOD_ROWS, 1, 6 * D)
    rope_tabs = _rope_tables(ls)
    pair_bias = _na_pair_bias(na_rpb)
    past = cache_na_k.shape[2]
    cache_k = cache_na_k.reshape(bs, -1, past, NA_HEADS * NA_DH)
    cache_v = cache_na_v.reshape(bs, -1, past, NA_HEADS * NA_DH)
    cache_kpe_slot = jnp.pad(cache_mla_kpe, ((0, 0), (0, 0), (0, 0), (NOPE, MLA_HP - QK_DIM)))
    ctx_row = lambda b: 0
    lat_row = lambda b: b + 1

    yp, ys = x_prompt, x_sample
    na_k, na_v, mla_ckv, mla_kpe = [], [], [], []
    for l in range(DEPTH):
        i = l // 2
        if l % 2 == 0:
            ap, att_p, kt, vt = _even_in_call(yp, mods, l, i, ctx_row, pp, True)
            na_k.append(kt.reshape(bp, NA_HEADS, NA_DH, lp))
            na_v.append(vt.reshape(bp, NA_HEADS, NA_DH, lp))
            a_s, qs, k_s, vs = _even_in_call(ys, mods, l, i, lat_row, pp, False)
            att_s = _na_call(qs, k_s, vs, cache_k, cache_v, i, pair_bias)
            w_out = pp["even_w_out"]
        else:
            ap, att_p, ckvp, kpep = _odd_in_call(yp, mods, l, i, ctx_row, pp, None, True)
            mla_ckv.append(ckvp)
            mla_kpe.append(kpep[:, :, NOPE:QK_DIM])
            a_s, qs, k_s, vs = _odd_in_call(ys, mods, l, i, lat_row, pp, rope_tabs, False)
            att_s = _mla_attn_call(qs, k_s, vs, cache_mla_ckv, cache_kpe_slot, i, pp)
            w_out = pp["odd_w_out"]
        yp = _out_mlp_call(yp.reshape(1, bp * lp, D), ap.reshape(1, bp * lp, A_W),
                           att_p.reshape(1, bp * lp, A_W), mods, l, ctx_row, w_out, i,
                           pp).reshape(bp, lp, D)
        ys = _out_mlp_call(ys, a_s, att_s, mods, l, lat_row, w_out, i, pp)
    state_k = jnp.transpose(jnp.stack(na_k, axis=0), (1, 0, 4, 2, 3))
    state_v = jnp.transpose(jnp.stack(na_v, axis=0), (1, 0, 4, 2, 3))
    return (yp, ys, state_k, state_v, jnp.stack(mla_ckv, axis=1), jnp.stack(mla_kpe, axis=1))
```
